```python
import math
import jax, jax.numpy as jnp
from jax import lax
import numpy as np

D_MODEL = 1024
BATCH = 4
SEQ = 8192
DEPTH = 1

ATTN_HEADS = 8
ATTN_KV_HEADS = 2
ATTN_GROUP = ATTN_HEADS // ATTN_KV_HEADS
HEAD_DIM = 64
WINDOW = 128
BLOCK = 128
NUM_BUCKETS = 32
MAX_DISTANCE = 128
GLA_HEADS = 4
GLA_KEY_DIM = D_MODEL // 2
GLA_VAL_DIM = D_MODEL
GLA_HEAD_K = GLA_KEY_DIM // GLA_HEADS
GLA_HEAD_V = GLA_VAL_DIM // GLA_HEADS
GLA_GATE_RANK = 16
GLA_GATE_TAU = 16.0
GLA_CHUNK = 64
D_FF = -(-8 * D_MODEL // (3 * 256)) * 256
PLE_DIM = 256
EPS = 1e-6

SPLITS = (ATTN_HEADS * HEAD_DIM, ATTN_KV_HEADS * HEAD_DIM, ATTN_KV_HEADS * HEAD_DIM,
          GLA_KEY_DIM, GLA_KEY_DIM, GLA_VAL_DIM, GLA_GATE_RANK, GLA_VAL_DIM,
          D_MODEL, D_MODEL)
D_IN = (ATTN_HEADS * HEAD_DIM + 2 * ATTN_KV_HEADS * HEAD_DIM + 2 * GLA_KEY_DIM
        + 2 * GLA_VAL_DIM + GLA_GATE_RANK + 2 * D_MODEL)

kernel_name = "hybrid_swa_gla_gated_merge"


def rms_norm(x, gain):
    xf = x.astype(jnp.float32)
    y = xf * lax.rsqrt(jnp.mean(xf * xf, axis=-1, keepdims=True) + EPS)
    return (y * gain.astype(jnp.float32)).astype(x.dtype)


def t5_bucket(dist):
    max_exact = NUM_BUCKETS // 2
    d_f = jnp.maximum(dist, max_exact).astype(jnp.float32)
    large = max_exact + (jnp.log(d_f / max_exact) / math.log(MAX_DISTANCE / max_exact)
                         * (NUM_BUCKETS - max_exact)).astype(jnp.int32)
    large = jnp.minimum(large, NUM_BUCKETS - 1)
    return jnp.where(dist < max_exact, dist, large)


def band_relative_bias(rel_table):
    q_loc = jnp.arange(BLOCK)[:, None] + BLOCK
    k_loc = jnp.arange(2 * BLOCK)[None, :]
    bucket = t5_bucket(jnp.maximum(q_loc - k_loc, 0))
    bias = rel_table.astype(jnp.float32)[bucket]
    return jnp.transpose(bias, (2, 0, 1)).reshape(ATTN_KV_HEADS, ATTN_GROUP, BLOCK, 2 * BLOCK)


def sliding_window_attention(q, k, v, sinks, rel_table):
    B, S = q.shape[0], q.shape[1]
    nb = S // BLOCK
    qb = q.astype(jnp.float32).reshape(B, nb, BLOCK, ATTN_KV_HEADS, ATTN_GROUP, HEAD_DIM)

    def band(t):
        t = jnp.pad(t.astype(jnp.float32), ((0, 0), (BLOCK, 0), (0, 0), (0, 0)))
        t = t.reshape(B, nb + 1, BLOCK, ATTN_KV_HEADS, HEAD_DIM)
        return jnp.concatenate([t[:, :-1], t[:, 1:]], axis=2)

    kb, vb = band(k), band(v)
    s = jnp.einsum('bnqhgd,bnkhd->bnhgqk', qb, kb) * (HEAD_DIM ** -0.5)
    s = s + band_relative_bias(rel_table)
    dist = (jnp.arange(BLOCK)[:, None] + BLOCK) - jnp.arange(2 * BLOCK)[None, :]
    in_window = (dist >= 0) & (dist < WINDOW)
    k_pos = jnp.arange(nb)[:, None] * BLOCK - BLOCK + jnp.arange(2 * BLOCK)[None, :]
    valid = in_window[None] & (k_pos >= 0)[:, None, :]
    s = jnp.where(valid[None, :, None, None], s, -1e30)
    sink = sinks.astype(jnp.float32).reshape(ATTN_KV_HEADS, ATTN_GROUP)[:, :, None, None]
    m = jnp.maximum(jnp.max(s, axis=-1, keepdims=True), sink)
    pr = jnp.exp(s - m)
    denom = jnp.sum(pr, axis=-1, keepdims=True) + jnp.exp(sink - m)
    o = jnp.einsum('bnhgqk,bnkhd->bnqhgd', pr / denom, vb)
    return o.reshape(B, S, ATTN_HEADS * HEAD_DIM)


def gla_chunked(q, k, v, log_a):
    B, S = q.shape[0], q.shape[1]
    nc = S // GLA_CHUNK

    def chunks(t, d):
        t = t.astype(jnp.float32).reshape(B, nc, GLA_CHUNK, GLA_HEADS, d)
        return t.transpose(1, 0, 3, 2, 4)

    qc = chunks(q, GLA_HEAD_K) * (GLA_HEAD_K ** -0.5)
    kc = chunks(k, GLA_HEAD_K)
    vc = chunks(v, GLA_HEAD_V)
    gc = jnp.cumsum(chunks(log_a, GLA_HEAD_K), axis=-2)
    causal = jnp.tril(jnp.ones((GLA_CHUNK, GLA_CHUNK), dtype=bool))[:, :, None]

    def step(state, inp):
        qi, ki, vi, gi = inp
        o_inter = jnp.einsum('bhtd,bhde->bhte', qi * jnp.exp(gi), state)
        decay = jnp.exp(jnp.where(causal, gi[:, :, :, None, :] - gi[:, :, None, :, :], -jnp.inf))
        scores = jnp.einsum('bhtsd,bhsd->bhts', qi[:, :, :, None, :] * decay, ki)
        o_intra = jnp.einsum('bhts,bhse->bhte', scores, vi)
        g_last = gi[:, :, -1:, :]
        k_dec = ki * jnp.exp(g_last - gi)
        state = state * jnp.exp(g_last[:, :, 0, :, None]) + jnp.einsum('bhsd,bhse->bhde', k_dec, vi)
        return state, o_inter + o_intra

    state0 = jnp.zeros((B, GLA_HEADS, GLA_HEAD_K, GLA_HEAD_V), jnp.float32)
    _, o = lax.scan(step, state0, (qc, kc, vc, gc))
    return o.transpose(1, 0, 3, 2, 4).reshape(B, S, GLA_HEADS, GLA_HEAD_V)


def hybrid_layer(x, p_i, w_in, w_gk2, b_gk, sinks, rel_table, w_proj_attn, w_proj_gla,
                 gla_norm, w_out, norm_mix, norm_ffn, w_ffn_in, w_ffn_out, norm_ple,
                 w_ple_gate, w_ple):
    B, S, _ = x.shape
    h = rms_norm(x, norm_mix)
    proj = h @ w_in
    points = np.cumsum(SPLITS)[:-1].tolist()
    q_a, k_a, v_a, q_g, k_g, v_g, gk_low, og, gate_a, gate_g = jnp.split(proj, points, axis=-1)

    o_a = sliding_window_attention(q_a.reshape(B, S, ATTN_HEADS, HEAD_DIM),
                                   k_a.reshape(B, S, ATTN_KV_HEADS, HEAD_DIM),
                                   v_a.reshape(B, S, ATTN_KV_HEADS, HEAD_DIM),
                                   sinks, rel_table)
    y_a = o_a.astype(x.dtype) @ w_proj_attn

    log_a = jax.nn.log_sigmoid((gk_low @ w_gk2 + b_gk).astype(jnp.float32)) / GLA_GATE_TAU
    o_g = gla_chunked(q_g.reshape(B, S, GLA_HEADS, GLA_HEAD_K),
                      k_g.reshape(B, S, GLA_HEADS, GLA_HEAD_K),
                      v_g.reshape(B, S, GLA_HEADS, GLA_HEAD_V),
                      log_a.reshape(B, S, GLA_HEADS, GLA_HEAD_K))
    o_g = rms_norm(o_g, gla_norm).reshape(B, S, GLA_VAL_DIM) * jax.nn.silu(og.astype(jnp.float32))
    y_g = o_g.astype(x.dtype) @ w_proj_gla

    mixed = jax.nn.sigmoid(gate_a) * y_a + jax.nn.sigmoid(gate_g) * y_g
    x = x + mixed @ w_out

    h = rms_norm(x, norm_ffn)
    g, u = jnp.split(h @ w_ffn_in, 2, axis=-1)
    x = x + (jax.nn.silu(g) * u) @ w_ffn_out

    h = rms_norm(x, norm_ple)
    x = x + jax.nn.sigmoid(h @ w_ple_gate) * (p_i.astype(x.dtype) @ w_ple)
    return x


def setup_inputs(seed: int = 0) -> dict:
    key = jax.random.key(seed)
    ks = jax.random.split(key, 20)
    f32 = jnp.float32

    def nrm(k, shape, scale):
        return jax.random.normal(k, shape, f32) * scale

    def gain(k, shape):
        return 1.0 + 0.01 * jax.random.normal(k, shape, f32)

    L = DEPTH
    return {
        "x": nrm(ks[0], (BATCH, SEQ, D_MODEL), 1.0),
        "p": nrm(ks[1], (DEPTH, BATCH, SEQ, PLE_DIM), 1.0),
        "w_in": nrm(ks[2], (L, D_MODEL, D_IN), D_MODEL ** -0.5),
        "w_gk2": nrm(ks[3], (L, GLA_GATE_RANK, GLA_KEY_DIM), GLA_GATE_RANK ** -0.5),
        "b_gk": nrm(ks[4], (L, GLA_KEY_DIM), 0.1),
        "sinks": nrm(ks[5], (L, ATTN_HEADS), 1.0),
        "rel_table": nrm(ks[6], (NUM_BUCKETS, ATTN_HEADS), 0.5),
        "w_proj_attn": nrm(ks[7], (L, ATTN_HEADS * HEAD_DIM, D_MODEL), (ATTN_HEADS * HEAD_DIM) ** -0.5),
        "w_proj_gla": nrm(ks[8], (L, GLA_VAL_DIM, D_MODEL), GLA_VAL_DIM ** -0.5),
        "gla_norm": gain(ks[9], (L, GLA_HEAD_V)),
        "w_out": nrm(ks[10], (L, D_MODEL, D_MODEL), D_MODEL ** -0.5),
        "norm_mix": gain(ks[11], (L, D_MODEL)),
        "norm_ffn": gain(ks[12], (L, D_MODEL)),
        "w_ffn_in": nrm(ks[13], (L, D_MODEL, 2 * D_FF), D_MODEL ** -0.5),
        "w_ffn_out": nrm(ks[14], (L, D_FF, D_MODEL), D_FF ** -0.5),
        "norm_ple": gain(ks[15], (L, D_MODEL)),
        "w_ple_gate": nrm(ks[16], (L, D_MODEL, D_MODEL), D_MODEL ** -0.5),
        "w_ple": nrm(ks[17], (L, PLE_DIM, D_MODEL), PLE_DIM ** -0.5),
        "norm_final": gain(ks[18], (D_MODEL,)),
    }


def reference(x, p, w_in, w_gk2, b_gk, sinks, rel_table, w_proj_attn, w_proj_gla, gla_norm,
              w_out, norm_mix, norm_ffn, w_ffn_in, w_ffn_out, norm_ple, w_ple_gate, w_ple,
              norm_final):
    for i in range(DEPTH):
        x = hybrid_layer(x, p[i], w_in[i], w_gk2[i], b_gk[i], sinks[i], rel_table,
                         w_proj_attn[i], w_proj_gla[i], gla_norm[i], w_out[i], norm_mix[i],
                         norm_ffn[i], w_ffn_in[i], w_ffn_out[i], norm_ple[i], w_ple_gate[i],
                         w_ple[i])
    return rms_norm(x, norm_final)
```

```python
import math
from functools import partial

import numpy as np
import jax
import jax.numpy as jnp
from jax import lax
from jax.experimental import pallas as pl
from jax.experimental.pallas import tpu as pltpu

F32 = jnp.float32
BF16 = jnp.bfloat16

D_MODEL = 1024
ATTN_HEADS = 8
ATTN_KV_HEADS = 2
ATTN_GROUP = ATTN_HEADS // ATTN_KV_HEADS
HEAD_DIM = 64
WINDOW = 128
BLOCK = 128
NUM_BUCKETS = 32
MAX_DISTANCE = 128
GLA_HEADS = 4
GLA_KEY_DIM = D_MODEL // 2
GLA_VAL_DIM = D_MODEL
GLA_HEAD_K = GLA_KEY_DIM // GLA_HEADS
GLA_HEAD_V = GLA_VAL_DIM // GLA_HEADS
GLA_GATE_RANK = 16
GLA_GATE_TAU = 16.0
D_FF = -(-8 * D_MODEL // (3 * 256)) * 256
PLE_DIM = 256
EPS = 1e-6
NEG = -1e30

LANES = 128
SUBLANES = 8

TM_PROJ = 512
GLA_CHUNK = 128
GLA_LEVELS = 7
ATTN_Q_ROWS = 512
TM_OUT = 256
FFN_COLS = 256
VMEM_LIMIT = 56 * 1024 * 1024

SEG_QKVA = (0, 768)
SEG_QKG = (768, 1792)
SEG_VG = (1792, 2816)
SEG_ACT = (2816, 5888)
N_MAIN = 5888


def _sigmoid(x):
    return 1.0 / (1.0 + jnp.exp(-x))


def _rms(x, gain):
    ms = jnp.mean(x * x, axis=-1, keepdims=True)
    return x * lax.rsqrt(ms + EPS) * gain


def _proj_body(x_ref, nm_ref, w_ref, wgk_ref, wgk2_ref, bgk_ref, tri_ref,
               qkva_ref, qkg_ref, vg_ref, g_ref, act_ref):
    h = _rms(x_ref[...], nm_ref[...]).astype(BF16)

    def proj(c0, c1):
        return jnp.dot(h, w_ref[:, c0:c1], preferred_element_type=F32)

    qkva_ref[:, 0:512] = proj(0, 512).astype(BF16)
    qkva_ref[:, 512:768] = proj(512, 768).astype(BF16)
    for c in range(2):
        a, b = 768 + 512 * c, 768 + 512 * (c + 1)
        qkg_ref[:, 512 * c:512 * (c + 1)] = proj(a, b).astype(BF16)
    for c in range(2):
        a, b = 1792 + 512 * c, 1792 + 512 * (c + 1)
        vg_ref[:, 512 * c:512 * (c + 1)] = proj(a, b).astype(BF16)
    for c in range(6):
        a, b = 2816 + 512 * c, 2816 + 512 * (c + 1)
        r = proj(a, b)
        s = _sigmoid(r)
        if c < 2:
            s = r * s
        act_ref[:, 512 * c:512 * (c + 1)] = s.astype(BF16)

    gk = jnp.dot(h, wgk_ref[...], preferred_element_type=F32).astype(BF16)
    z = jnp.dot(gk, wgk2_ref[...], preferred_element_type=F32) + bgk_ref[...]
    la = (jnp.minimum(z, 0.0) - jnp.log(1.0 + jnp.exp(-jnp.abs(z)))) * (1.0 / GLA_GATE_TAU)
    hi = la.astype(BF16)
    r1 = la - hi.astype(F32)
    mid = r1.astype(BF16)
    lo = (r1 - mid.astype(F32)).astype(BF16)
    tri = tri_ref[...]
    C = GLA_CHUNK
    for j in range(TM_PROJ // C):
        rows = slice(j * C, (j + 1) * C)
        acc = jnp.dot(tri, hi[rows], preferred_element_type=F32)
        acc = acc + jnp.dot(tri, mid[rows], preferred_element_type=F32)
        acc = acc + jnp.dot(tri, lo[rows], preferred_element_type=F32)
        g_ref[rows, :] = acc


def _proj_call(x2, nm, w_main, w_gk, w_gk2, b_gk, tri):
    T = x2.shape[0]
    tm = TM_PROJ
    full = lambda a: pl.BlockSpec(a.shape, lambda i: (0,) * a.ndim)
    rows = lambda n: pl.BlockSpec((tm, n), lambda i: (i, 0))
    return pl.pallas_call(
        _proj_body,
        grid=(T // tm,),
        in_specs=[rows(D_MODEL), full(nm), full(w_main), full(w_gk), full(w_gk2), full(b_gk), full(tri)],
        out_specs=[rows(768), rows(1024), rows(1024), rows(512), rows(3072)],
        out_shape=[
            jax.ShapeDtypeStruct((T, 768), BF16),
            jax.ShapeDtypeStruct((T, 1024), BF16),
            jax.ShapeDtypeStruct((T, 1024), BF16),
            jax.ShapeDtypeStruct((T, 512), F32),
            jax.ShapeDtypeStruct((T, 3072), BF16),
        ],
        compiler_params=pltpu.CompilerParams(
            dimension_semantics=("arbitrary",), vmem_limit_bytes=VMEM_LIMIT),
        name="in_proj",
    )(x2, nm, w_main, w_gk, w_gk2, b_gk, tri)


def _attn_body(q_ref, kp_ref, kc_ref, vp_ref, vc_ref, bias_ref, sink_ref, o_ref):
    n = pl.program_id(1)
    nq = ATTN_Q_ROWS // BLOCK
    lane_k = lax.broadcasted_iota(jnp.int32, (2 * BLOCK, LANES), 1)
    col = lax.broadcasted_iota(jnp.int32, (ATTN_GROUP * BLOCK, 2 * BLOCK), 1)
    lane_o = lax.broadcasted_iota(jnp.int32, (ATTN_GROUP * BLOCK, LANES), 1)
    for j in range(nq):
        r0 = j * BLOCK
        q = q_ref[0, r0:r0 + BLOCK, :]
        qall = jnp.concatenate([q[:, g * LANES:(g + 1) * LANES] for g in range(ATTN_GROUP)], axis=0)
        if j == 0:
            kband = jnp.concatenate([kp_ref[0], kc_ref[0, 0:BLOCK, :]], axis=0)
            vband = jnp.concatenate([vp_ref[0], vc_ref[0, 0:BLOCK, :]], axis=0)
        else:
            kband = kc_ref[0, r0 - BLOCK:r0 + BLOCK, :]
            vband = vc_ref[0, r0 - BLOCK:r0 + BLOCK, :]
        zero = jnp.zeros_like(kband)
        ksel = (jnp.where(lane_k < HEAD_DIM, kband, zero), jnp.where(lane_k < HEAD_DIM, zero, kband))
        outs = []
        for kv in range(ATTN_KV_HEADS):
            s = lax.dot_general(qall, ksel[kv], (((1,), (1,)), ((), ())), preferred_element_type=F32)
            s = s + bias_ref[kv]
            if j == 0:
                s = jnp.where(jnp.logical_and(n == 0, col < BLOCK), NEG, s)
            sink = sink_ref[kv][:, 0:1]
            m = jnp.maximum(jnp.max(s, axis=-1, keepdims=True), sink)
            p = jnp.exp(s - m)
            den = jnp.sum(p, axis=-1, keepdims=True) + jnp.exp(sink - m)
            o = jnp.dot(p.astype(BF16), vband, preferred_element_type=F32)
            outs.append(o / den)
        o = jnp.where(lane_o < HEAD_DIM, outs[0], outs[1])
        for g in range(ATTN_GROUP):
            o_ref[0, r0:r0 + BLOCK, g * LANES:(g + 1) * LANES] = o[g * BLOCK:(g + 1) * BLOCK].astype(BF16)


def _attn_call(qkva, bias, sink):
    B, S, _ = qkva.shape
    tq = ATTN_Q_ROWS
    nprev = tq // BLOCK
    return pl.pallas_call(
        _attn_body,
        grid=(B, S // tq),
        in_specs=[
            pl.BlockSpec((1, tq, 512), lambda b, n: (b, n, 0)),
            pl.BlockSpec((1, BLOCK, LANES), lambda b, n: (b, jnp.maximum(n * nprev - 1, 0), 4)),
            pl.BlockSpec((1, tq, LANES), lambda b, n: (b, n, 4)),
            pl.BlockSpec((1, BLOCK, LANES), lambda b, n: (b, jnp.maximum(n * nprev - 1, 0), 5)),
            pl.BlockSpec((1, tq, LANES), lambda b, n: (b, n, 5)),
            pl.BlockSpec(bias.shape, lambda b, n: (0, 0, 0)),
            pl.BlockSpec(sink.shape, lambda b, n: (0, 0, 0)),
        ],
        out_specs=pl.BlockSpec((1, tq, 512), lambda b, n: (b, n, 0)),
        out_shape=jax.ShapeDtypeStruct((B, S, 512), BF16),
        compiler_params=pltpu.CompilerParams(
            dimension_semantics=("arbitrary", "arbitrary"), vmem_limit_bytes=VMEM_LIMIT),
        name="swa_attn",
    )(qkva, qkva, qkva, qkva, qkva, bias, sink)


def _anchor(g_ref, col0, b):
    C = GLA_CHUNK
    cols = slice(col0, col0 + LANES)

    def row(r, n):
        return jnp.broadcast_to(g_ref[0, r:r + 1, cols], (n, LANES))

    if 2 * b >= SUBLANES:
        parts = [row(p * 2 * b + b - 1, 2 * b) for p in range(C // (2 * b))]
        return jnp.concatenate(parts, axis=0)
    sub = lax.broadcasted_iota(jnp.int32, (SUBLANES, LANES), 0)
    parts = []
    for i in range(C // SUBLANES):
        r0 = i * SUBLANES
        if b == 2:
            a = jnp.where(sub < 4, row(r0 + 1, SUBLANES), row(r0 + 5, SUBLANES))
        else:
            a = jnp.where(sub < 2, row(r0, SUBLANES),
                          jnp.where(sub < 4, row(r0 + 2, SUBLANES),
                                    jnp.where(sub < 6, row(r0 + 4, SUBLANES), row(r0 + 6, SUBLANES))))
        parts.append(a)
    return jnp.concatenate(parts, axis=0)


def _gla_body(qk_ref, v_ref, g_ref, lv_ref, o_ref, s_scr):
    C = GLA_CHUNK

    @pl.when(pl.program_id(1) == 0)
    def _():
        s_scr[...] = jnp.zeros_like(s_scr)

    lv = lv_ref[...]
    for h in range(GLA_HEADS):
        kc0 = h * GLA_HEAD_K
        q_b = qk_ref[0, :, kc0:kc0 + GLA_HEAD_K]
        k_b = qk_ref[0, :, GLA_KEY_DIM + kc0:GLA_KEY_DIM + kc0 + GLA_HEAD_K]
        q = q_b.astype(F32)
        k = k_b.astype(F32)
        v = v_ref[0, :, h * GLA_HEAD_V:(h + 1) * GLA_HEAD_V]
        g = g_ref[0, :, kc0:kc0 + GLA_HEAD_K]
        glast = jnp.broadcast_to(g_ref[0, C - 1:C, kc0:kc0 + GLA_HEAD_K], (C, GLA_HEAD_K))

        scores = jnp.where(lv == GLA_LEVELS,
                           lax.dot_general(q_b, k_b, (((1,), (1,)), ((), ())), preferred_element_type=F32),
                           0.0)
        for l in range(GLA_LEVELS):
            d = g - _anchor(g_ref, kc0, 1 << l)
            e = jnp.exp(jnp.minimum(d, -d))
            s_l = lax.dot_general((q * e).astype(BF16), (k * e).astype(BF16),
                                  (((1,), (1,)), ((), ())), preferred_element_type=F32)
            scores = jnp.where(lv == l, s_l, scores)

        state = s_scr[h]
        qt = (q * jnp.exp(g)).astype(BF16)
        lhs = jnp.concatenate([qt, scores.astype(BF16)], axis=1)
        rhs = jnp.concatenate([state.astype(BF16), v], axis=0)
        o = jnp.dot(lhs, rhs, preferred_element_type=F32)
        o_ref[0, :, h * GLA_HEAD_V:(h + 1) * GLA_HEAD_V] = o.astype(BF16)

        kd_t = (k * jnp.exp(glast - g)).T.astype(BF16)
        gam = jnp.exp(glast.T)
        upd = jnp.dot(kd_t, v, preferred_element_type=F32)
        s_scr[h] = state * jnp.concatenate([gam, gam], axis=1) + upd


def _gla_call(qkg, vg, g, lv):
    B, S, _ = qkg.shape
    C = GLA_CHUNK
    return pl.pallas_call(
        _gla_body,
        grid=(B, S // C),
        in_specs=[
            pl.BlockSpec((1, C, 1024), lambda b, c: (b, c, 0)),
            pl.BlockSpec((1, C, 1024), lambda b, c: (b, c, 0)),
            pl.BlockSpec((1, C, 512), lambda b, c: (b, c, 0)),
            pl.BlockSpec((C, C), lambda b, c: (0, 0)),
        ],
        out_specs=pl.BlockSpec((1, C, 1024), lambda b, c: (b, c, 0)),
        out_shape=jax.ShapeDtypeStruct((B, S, 1024), BF16),
        scratch_shapes=[pltpu.VMEM((GLA_HEADS, GLA_HEAD_K, GLA_HEAD_V), F32)],
        compiler_params=pltpu.CompilerParams(
            dimension_semantics=("arbitrary", "arbitrary"), vmem_limit_bytes=VMEM_LIMIT),
        name="gla",
    )(qkg, vg, g, lv)


def _out_body(x_ref, p_ref, oa_ref, og_ref, act_ref,
              wpa_ref, wpg_ref, wo_ref, wfi_ref, wfo_ref, wpgate_ref, wple_ref,
              gn_ref, nffn_ref, nple_ref, nfin_ref, out_ref, a_scr):
    ya = jnp.dot(oa_ref[...], wpa_ref[...], preferred_element_type=F32)
    parts = []
    for h in range(GLA_HEADS):
        cols = slice(h * GLA_HEAD_V, (h + 1) * GLA_HEAD_V)
        oh = _rms(og_ref[:, cols].astype(F32), gn_ref[...]) * act_ref[:, cols].astype(F32)
        parts.append(oh.astype(BF16))
    yg = jnp.dot(jnp.concatenate(parts, axis=1), wpg_ref[...], preferred_element_type=F32)
    mixed = (act_ref[:, 1024:2048].astype(F32) * ya + act_ref[:, 2048:3072].astype(F32) * yg)
    x1 = x_ref[...] + jnp.dot(mixed.astype(BF16), wo_ref[...], preferred_element_type=F32)

    h2 = _rms(x1, nffn_ref[...]).astype(BF16)
    w = FFN_COLS
    for c in range(D_FF // w):
        gu = jnp.dot(h2, wfi_ref[:, 2 * w * c:2 * w * (c + 1)], preferred_element_type=F32)
        gg = gu[:, :w]
        a_scr[:, w * c:w * (c + 1)] = (gg * _sigmoid(gg) * gu[:, w:]).astype(BF16)
    x2 = x1 + jnp.dot(a_scr[...], wfo_ref[...], preferred_element_type=F32)

    h3 = _rms(x2, nple_ref[...]).astype(BF16)
    gate = _sigmoid(jnp.dot(h3, wpgate_ref[...], preferred_element_type=F32))
    pe = jnp.dot(p_ref[...].astype(BF16), wple_ref[...], preferred_element_type=F32)
    x3 = x2 + gate * pe
    out_ref[...] = _rms(x3, nfin_ref[...])


def _out_call(x2, p2, oa, og, act, weights, gains):
    T = x2.shape[0]
    tm = TM_OUT
    full = lambda a: pl.BlockSpec(a.shape, lambda i: (0,) * a.ndim)
    rows = lambda n: pl.BlockSpec((tm, n), lambda i: (i, 0))
    return pl.pallas_call(
        _out_body,
        grid=(T // tm,),
        in_specs=[rows(D_MODEL), rows(PLE_DIM), rows(512), rows(1024), rows(3072)]
                 + [full(w) for w in weights] + [full(gn) for gn in gains],
        out_specs=rows(D_MODEL),
        out_shape=jax.ShapeDtypeStruct((T, D_MODEL), F32),
        scratch_shapes=[pltpu.VMEM((tm, D_FF), BF16)],
        compiler_params=pltpu.CompilerParams(
            dimension_semantics=("arbitrary",), vmem_limit_bytes=VMEM_LIMIT),
        name="merge_ffn",
    )(x2, p2, oa, og, act, *weights, *gains)


def _t5_bucket_table():
    q_loc = np.arange(BLOCK)[:, None] + BLOCK
    k_loc = np.arange(2 * BLOCK)[None, :]
    dist = np.maximum(q_loc - k_loc, 0)
    max_exact = NUM_BUCKETS // 2
    d_f = np.maximum(dist, max_exact).astype(np.float32)
    large = max_exact + (np.log(d_f / max_exact) / math.log(MAX_DISTANCE / max_exact)
                         * (NUM_BUCKETS - max_exact)).astype(np.int32)
    large = np.minimum(large, NUM_BUCKETS - 1)
    bucket = np.where(dist < max_exact, dist, large)
    raw = q_loc - k_loc
    in_window = (raw >= 0) & (raw < WINDOW)
    return bucket.astype(np.int32), in_window


def _level_table():
    r = np.arange(GLA_CHUNK)[:, None]
    c = np.arange(GLA_CHUNK)[None, :]
    x = np.bitwise_xor(r, c)
    lv = np.where(x > 0, np.floor(np.log2(np.maximum(x, 1))).astype(np.int32), GLA_LEVELS)
    return np.where(r >= c, lv, -1).astype(np.int32)


def _head_pair_perm():
    idx = []
    for g in range(ATTN_GROUP):
        idx += list(range(g * HEAD_DIM, (g + 1) * HEAD_DIM))
        idx += list(range((ATTN_GROUP + g) * HEAD_DIM, (ATTN_GROUP + g + 1) * HEAD_DIM))
    return np.asarray(idx, dtype=np.int32)


def kernel(x, p, w_in, w_gk2, b_gk, sinks, rel_table, w_proj_attn, w_proj_gla, gla_norm, w_out,
           norm_mix, norm_ffn, w_ffn_in, w_ffn_out, norm_ple, w_ple_gate, w_ple, norm_final):
    B, S, D = x.shape
    T = B * S
    assert w_in.shape[0] == 1, "single-layer trunk only"
    i = 0
    perm = _head_pair_perm()
    bucket, in_window = _t5_bucket_table()
    lv = jnp.asarray(_level_table())
    tri = jnp.asarray(np.tril(np.ones((GLA_CHUNK, GLA_CHUNK), np.float32)), dtype=BF16)

    bias = jnp.transpose(rel_table.astype(F32)[bucket], (2, 0, 1))
    bias = jnp.where(in_window[None], bias, NEG).reshape(ATTN_KV_HEADS, ATTN_GROUP * BLOCK, 2 * BLOCK)

    xs = x.reshape(T, D)
    w = w_in[i]
    w_main = jnp.concatenate([
        w[:, 0:512][:, perm] * (HEAD_DIM ** -0.5),
        w[:, 512:768],
        w[:, 768:1280] * (GLA_HEAD_K ** -0.5),
        w[:, 1280:2816],
        w[:, 2832:5904],
    ], axis=1).astype(BF16)
    w_gk = jnp.pad(w[:, 2816:2832], ((0, 0), (0, LANES - GLA_GATE_RANK))).astype(BF16)
    w_gk2p = jnp.pad(w_gk2[i], ((0, LANES - GLA_GATE_RANK), (0, 0))).astype(BF16)
    sink = jnp.broadcast_to(
        jnp.repeat(sinks[i].astype(F32).reshape(ATTN_KV_HEADS, ATTN_GROUP), BLOCK, axis=1)[:, :, None],
        (ATTN_KV_HEADS, ATTN_GROUP * BLOCK, LANES))

    qkva, qkg, vg, g, act = _proj_call(
        xs, norm_mix[i].reshape(1, D), w_main, w_gk, w_gk2p, b_gk[i].reshape(1, -1), tri)

    oa = _attn_call(qkva.reshape(B, S, 768), bias, sink).reshape(T, 512)
    og = _gla_call(qkg.reshape(B, S, 1024), vg.reshape(B, S, 1024), g.reshape(B, S, 512), lv).reshape(T, 1024)

    wfi = w_ffn_in[i]
    nc = D_FF // FFN_COLS
    wfi = jnp.concatenate(
        [wfi[:, :D_FF].reshape(D, nc, FFN_COLS), wfi[:, D_FF:].reshape(D, nc, FFN_COLS)],
        axis=2).reshape(D, 2 * D_FF)
    weights = [
        w_proj_attn[i][perm].astype(BF16), w_proj_gla[i].astype(BF16), w_out[i].astype(BF16),
        wfi.astype(BF16), w_ffn_out[i].astype(BF16), w_ple_gate[i].astype(BF16), w_ple[i].astype(BF16),
    ]
    gains = [gla_norm[i].reshape(1, -1), norm_ffn[i].reshape(1, D), norm_ple[i].reshape(1, D),
             norm_final.reshape(1, D)]
    out = _out_call(xs, p[i].reshape(T, PLE_DIM), oa, og, act, weights, gains)
    return out.reshape(B, S, D)
```

```python
import math
from functools import partial

import numpy as np
import jax
import jax.numpy as jnp
from jax import lax
from jax.experimental import pallas as pl
from jax.experimental.pallas import tpu as pltpu

F32 = jnp.float32
BF16 = jnp.bfloat16

D_MODEL = 1024
ATTN_HEADS = 8
ATTN_KV_HEADS = 2
ATTN_GROUP = ATTN_HEADS // ATTN_KV_HEADS
HEAD_DIM = 64
WINDOW = 128
BLOCK = 128
NUM_BUCKETS = 32
MAX_DISTANCE = 128
GLA_HEADS = 4
GLA_KEY_DIM = D_MODEL // 2
GLA_VAL_DIM = D_MODEL
GLA_HEAD_K = GLA_KEY_DIM // GLA_HEADS
GLA_HEAD_V = GLA_VAL_DIM // GLA_HEADS
GLA_GATE_RANK = 16
GLA_GATE_TAU = 16.0
D_FF = -(-8 * D_MODEL // (3 * 256)) * 256
PLE_DIM = 256
EPS = 1e-6
NEG = -1e30

LANES = 128
SUBLANES = 8

TM_PROJ = 512
GLA_CHUNK = 128
GLA_LEVELS = 7
ATTN_Q_ROWS = 512
TM_OUT = 256
FFN_COLS = 256
VMEM_LIMIT = 56 * 1024 * 1024

SEG_QKVA = (0, 768)
SEG_QKG = (768, 1792)
SEG_VG = (1792, 2816)
SEG_ACT = (2816, 5888)
N_MAIN = 5888


def _sigmoid(x):
    return 1.0 / (1.0 + jnp.exp(-x))


def _rms(x, gain):
    ms = jnp.mean(x * x, axis=-1, keepdims=True)
    return x * lax.rsqrt(ms + EPS) * gain


def _proj_body(x_ref, nm_ref, w_ref, wgk_ref, wgk2_ref, bgk_ref, tri_ref,
               qkva_ref, qkg_ref, vg_ref, g_ref, act_ref):
    h = _rms(x_ref[...], nm_ref[...]).astype(BF16)

    def proj(c0, c1):
        return jnp.dot(h, w_ref[:, c0:c1], preferred_element_type=F32)

    qkva_ref[:, 0:512] = proj(0, 512).astype(BF16)
    qkva_ref[:, 512:768] = proj(512, 768).astype(BF16)
    for c in range(2):
        a, b = 768 + 512 * c, 768 + 512 * (c + 1)
        qkg_ref[:, 512 * c:512 * (c + 1)] = proj(a, b).astype(BF16)
    for c in range(2):
        a, b = 1792 + 512 * c, 1792 + 512 * (c + 1)
        vg_ref[:, 512 * c:512 * (c + 1)] = proj(a, b).astype(BF16)
    for c in range(6):
        a, b = 2816 + 512 * c, 2816 + 512 * (c + 1)
        r = proj(a, b)
        s = _sigmoid(r)
        if c < 2:
            s = r * s
        act_ref[:, 512 * c:512 * (c + 1)] = s.astype(BF16)

    gk = jnp.dot(h, wgk_ref[...], preferred_element_type=F32).astype(BF16)
    z = jnp.dot(gk, wgk2_ref[...], preferred_element_type=F32) + bgk_ref[...]
    la = (jnp.minimum(z, 0.0) - jnp.log(1.0 + jnp.exp(-jnp.abs(z)))) * (1.0 / GLA_GATE_TAU)
    hi = la.astype(BF16)
    r1 = la - hi.astype(F32)
    mid = r1.astype(BF16)
    lo = (r1 - mid.astype(F32)).astype(BF16)
    tri = tri_ref[...]
    C = GLA_CHUNK
    for j in range(TM_PROJ // C):
        rows = slice(j * C, (j + 1) * C)
        acc = jnp.dot(tri, hi[rows], preferred_element_type=F32)
        acc = acc + jnp.dot(tri, mid[rows], preferred_element_type=F32)
        acc = acc + jnp.dot(tri, lo[rows], preferred_element_type=F32)
        g_ref[rows, :] = acc


def _proj_call(x2, nm, w_main, w_gk, w_gk2, b_gk, tri):
    T = x2.shape[0]
    tm = TM_PROJ
    full = lambda a: pl.BlockSpec(a.shape, lambda i: (0,) * a.ndim)
    rows = lambda n: pl.BlockSpec((tm, n), lambda i: (i, 0))
    return pl.pallas_call(
        _proj_body,
        grid=(T // tm,),
        in_specs=[rows(D_MODEL), full(nm), full(w_main), full(w_gk), full(w_gk2), full(b_gk), full(tri)],
        out_specs=[rows(768), rows(1024), rows(1024), rows(512), rows(3072)],
        out_shape=[
            jax.ShapeDtypeStruct((T, 768), BF16),
            jax.ShapeDtypeStruct((T, 1024), BF16),
            jax.ShapeDtypeStruct((T, 1024), BF16),
            jax.ShapeDtypeStruct((T, 512), F32),
            jax.ShapeDtypeStruct((T, 3072), BF16),
        ],
        compiler_params=pltpu.CompilerParams(
            dimension_semantics=("arbitrary",), vmem_limit_bytes=VMEM_LIMIT),
        name="in_proj",
    )(x2, nm, w_main, w_gk, w_gk2, b_gk, tri)


def _attn_body(q_ref, kp_ref, kc_ref, vp_ref, vc_ref, bias_ref, o_ref):
    first = pl.program_id(1) == 0
    nq = ATTN_Q_ROWS // BLOCK
    band_row = lax.broadcasted_iota(jnp.int32, (2 * BLOCK, LANES), 0)
    low = lax.broadcasted_iota(jnp.int32, (2 * BLOCK, LANES), 1) < HEAD_DIM
    col = lax.broadcasted_iota(jnp.int32, (BLOCK, 2 * BLOCK), 1)
    no_prev = jnp.logical_and(first, jnp.logical_and(col >= 1, col < BLOCK))
    low_o = lax.broadcasted_iota(jnp.int32, (BLOCK, LANES), 1) < HEAD_DIM
    ones = jnp.ones((2 * BLOCK, LANES), BF16)
    nt = (((1,), (1,)), ((), ()))
    for j in range(nq):
        r0 = j * BLOCK
        if j == 0:
            kband = jnp.concatenate([kp_ref[0], kc_ref[0, 0:BLOCK, :]], axis=0)
            vband = jnp.concatenate([vp_ref[0], vc_ref[0, 0:BLOCK, :]], axis=0)
        else:
            kband = kc_ref[0, r0 - BLOCK:r0 + BLOCK, :]
            vband = vc_ref[0, r0 - BLOCK:r0 + BLOCK, :]
        zero = jnp.zeros_like(kband)
        kband = jnp.where(band_row == 0, zero, kband)
        vband = jnp.where(band_row == 0, zero, vband)
        kroll = pltpu.roll(kband, HEAD_DIM, 1)
        vroll = pltpu.roll(vband, HEAD_DIM, 1)
        ktile = ((jnp.where(low, kband, zero), jnp.where(low, zero, kroll)),
                 (jnp.where(low, kroll, zero), jnp.where(low, zero, kband)))
        probs = [[None, None], [None, None]]
        for kv in range(ATTN_KV_HEADS):
            qg = q_ref[0, r0:r0 + BLOCK, 2 * kv * LANES:(2 * kv + 2) * LANES]
            qg = jnp.concatenate([qg[:, :LANES], qg[:, LANES:]], axis=0)
            for odd in range(2):
                s2 = lax.dot_general(qg, ktile[kv][odd], nt, preferred_element_type=F32)
                ps = []
                for half in range(2):
                    head = 2 * (2 * kv + half) + odd
                    s = s2[half * BLOCK:(half + 1) * BLOCK] + bias_ref[head]
                    if j == 0:
                        s = jnp.where(no_prev, NEG, s)
                    ps.append(jnp.exp(s - jnp.max(s, axis=-1, keepdims=True)).astype(BF16))
                probs[kv][odd] = jnp.concatenate(ps, axis=0)
        vnat = jnp.concatenate([vband, ones], axis=1)
        vrol = jnp.concatenate([vroll, ones], axis=1)
        od_nat = jnp.dot(jnp.concatenate([probs[0][0], probs[1][1]], axis=0), vnat, preferred_element_type=F32)
        od_rol = jnp.dot(jnp.concatenate([probs[0][1], probs[1][0]], axis=0), vrol, preferred_element_type=F32)
        for blk in range(ATTN_HEADS // 2):
            rows = slice(blk * BLOCK, (blk + 1) * BLOCK)
            o_nat = od_nat[rows, :LANES] / od_nat[rows, LANES:]
            o_rol = od_rol[rows, :LANES] / od_rol[rows, LANES:]
            even, odd_ = (o_nat, o_rol) if blk < ATTN_HEADS // 4 else (o_rol, o_nat)
            o_ref[0, r0:r0 + BLOCK, blk * LANES:(blk + 1) * LANES] = jnp.where(low_o, even, odd_).astype(BF16)


def _attn_call(qkva, bias):
    B, S, _ = qkva.shape
    tq = ATTN_Q_ROWS
    nprev = tq // BLOCK
    return pl.pallas_call(
        _attn_body,
        grid=(B, S // tq),
        in_specs=[
            pl.BlockSpec((1, tq, 512), lambda b, n: (b, n, 0)),
            pl.BlockSpec((1, BLOCK, LANES), lambda b, n: (b, jnp.maximum(n * nprev - 1, 0), 4)),
            pl.BlockSpec((1, tq, LANES), lambda b, n: (b, n, 4)),
            pl.BlockSpec((1, BLOCK, LANES), lambda b, n: (b, jnp.maximum(n * nprev - 1, 0), 5)),
            pl.BlockSpec((1, tq, LANES), lambda b, n: (b, n, 5)),
            pl.BlockSpec(bias.shape, lambda b, n: (0, 0, 0)),
        ],
        out_specs=pl.BlockSpec((1, tq, 512), lambda b, n: (b, n, 0)),
        out_shape=jax.ShapeDtypeStruct((B, S, 512), BF16),
        compiler_params=pltpu.CompilerParams(
            dimension_semantics=("arbitrary", "arbitrary"), vmem_limit_bytes=VMEM_LIMIT),
        name="swa_attn",
    )(qkva, qkva, qkva, qkva, qkva, bias)


def _anchor(g_ref, col0, b):
    C = GLA_CHUNK
    cols = slice(col0, col0 + LANES)

    def row(r, n):
        return jnp.broadcast_to(g_ref[0, r:r + 1, cols], (n, LANES))

    if 2 * b >= SUBLANES:
        parts = [row(p * 2 * b + b - 1, 2 * b) for p in range(C // (2 * b))]
        return jnp.concatenate(parts, axis=0)
    sub = lax.broadcasted_iota(jnp.int32, (SUBLANES, LANES), 0)
    parts = []
    for i in range(C // SUBLANES):
        r0 = i * SUBLANES
        if b == 2:
            a = jnp.where(sub < 4, row(r0 + 1, SUBLANES), row(r0 + 5, SUBLANES))
        else:
            a = jnp.where(sub < 2, row(r0, SUBLANES),
                          jnp.where(sub < 4, row(r0 + 2, SUBLANES),
                                    jnp.where(sub < 6, row(r0 + 4, SUBLANES), row(r0 + 6, SUBLANES))))
        parts.append(a)
    return jnp.concatenate(parts, axis=0)


def _gla_body(qk_ref, v_ref, g_ref, lv_ref, o_ref, s_scr):
    C = GLA_CHUNK

    @pl.when(pl.program_id(1) == 0)
    def _():
        s_scr[...] = jnp.zeros_like(s_scr)

    lv = lv_ref[...]
    for h in range(GLA_HEADS):
        kc0 = h * GLA_HEAD_K
        q_b = qk_ref[0, :, kc0:kc0 + GLA_HEAD_K]
        k_b = qk_ref[0, :, GLA_KEY_DIM + kc0:GLA_KEY_DIM + kc0 + GLA_HEAD_K]
        q = q_b.astype(F32)
        k = k_b.astype(F32)
        v = v_ref[0, :, h * GLA_HEAD_V:(h + 1) * GLA_HEAD_V]
        g = g_ref[0, :, kc0:kc0 + GLA_HEAD_K]
        glast = jnp.broadcast_to(g_ref[0, C - 1:C, kc0:kc0 + GLA_HEAD_K], (C, GLA_HEAD_K))

        scores = jnp.where(lv == GLA_LEVELS,
                           lax.dot_general(q_b, k_b, (((1,), (1,)), ((), ())), preferred_element_type=F32),
                           0.0)
        for l in range(GLA_LEVELS):
            d = g - _anchor(g_ref, kc0, 1 << l)
            e = jnp.exp(jnp.minimum(d, -d))
            s_l = lax.dot_general((q * e).astype(BF16), (k * e).astype(BF16),
                                  (((1,), (1,)), ((), ())), preferred_element_type=F32)
            scores = jnp.where(lv == l, s_l, scores)

        state = s_scr[h]
        qt = (q * jnp.exp(g)).astype(BF16)
        lhs = jnp.concatenate([qt, scores.astype(BF16)], axis=1)
        rhs = jnp.concatenate([state.astype(BF16), v], axis=0)
        o = jnp.dot(lhs, rhs, preferred_element_type=F32)
        o_ref[0, :, h * GLA_HEAD_V:(h + 1) * GLA_HEAD_V] = o.astype(BF16)

        kd_t = (k * jnp.exp(glast - g)).T.astype(BF16)
        gam = jnp.exp(glast.T)
        upd = jnp.dot(kd_t, v, preferred_element_type=F32)
        s_scr[h] = state * jnp.concatenate([gam, gam], axis=1) + upd


def _gla_call(qkg, vg, g, lv):
    B, S, _ = qkg.shape
    C = GLA_CHUNK
    return pl.pallas_call(
        _gla_body,
        grid=(B, S // C),
        in_specs=[
            pl.BlockSpec((1, C, 1024), lambda b, c: (b, c, 0)),
            pl.BlockSpec((1, C, 1024), lambda b, c: (b, c, 0)),
            pl.BlockSpec((1, C, 512), lambda b, c: (b, c, 0)),
            pl.BlockSpec((C, C), lambda b, c: (0, 0)),
        ],
        out_specs=pl.BlockSpec((1, C, 1024), lambda b, c: (b, c, 0)),
        out_shape=jax.ShapeDtypeStruct((B, S, 1024), BF16),
        scratch_shapes=[pltpu.VMEM((GLA_HEADS, GLA_HEAD_K, GLA_HEAD_V), F32)],
        compiler_params=pltpu.CompilerParams(
            dimension_semantics=("arbitrary", "arbitrary"), vmem_limit_bytes=VMEM_LIMIT),
        name="gla",
    )(qkg, vg, g, lv)


def _out_body(x_ref, p_ref, oa_ref, og_ref, act_ref,
              wpa_ref, wpg_ref, wo_ref, wfi_ref, wfo_ref, wpgate_ref, wple_ref,
              gn_ref, nffn_ref, nple_ref, nfin_ref, out_ref, a_scr):
    ya = jnp.dot(oa_ref[...], wpa_ref[...], preferred_element_type=F32)
    parts = []
    for h in range(GLA_HEADS):
        cols = slice(h * GLA_HEAD_V, (h + 1) * GLA_HEAD_V)
        oh = _rms(og_ref[:, cols].astype(F32), gn_ref[...]) * act_ref[:, cols].astype(F32)
        parts.append(oh.astype(BF16))
    yg = jnp.dot(jnp.concatenate(parts, axis=1), wpg_ref[...], preferred_element_type=F32)
    mixed = (act_ref[:, 1024:2048].astype(F32) * ya + act_ref[:, 2048:3072].astype(F32) * yg)
    x1 = x_ref[...] + jnp.dot(mixed.astype(BF16), wo_ref[...], preferred_element_type=F32)

    h2 = _rms(x1, nffn_ref[...]).astype(BF16)
    w = FFN_COLS
    for c in range(D_FF // w):
        w_gu = jnp.concatenate(
            [wfi_ref[:, w * c:w * (c + 1)], wfi_ref[:, D_FF + w * c:D_FF + w * (c + 1)]], axis=1)
        gu = jnp.dot(h2, w_gu, preferred_element_type=F32)
        gg = gu[:, :w]
        a_scr[:, w * c:w * (c + 1)] = (gg * _sigmoid(gg) * gu[:, w:]).astype(BF16)
    x2 = x1 + jnp.dot(a_scr[...], wfo_ref[...], preferred_element_type=F32)

    h3 = _rms(x2, nple_ref[...]).astype(BF16)
    gate = _sigmoid(jnp.dot(h3, wpgate_ref[...], preferred_element_type=F32))
    pe = jnp.dot(p_ref[...].astype(BF16), wple_ref[...], preferred_element_type=F32)
    x3 = x2 + gate * pe
    out_ref[...] = _rms(x3, nfin_ref[...])


def _out_call(x2, p2, oa, og, act, weights, gains):
    T = x2.shape[0]
    tm = TM_OUT
    full = lambda a: pl.BlockSpec(a.shape, lambda i: (0,) * a.ndim)
    rows = lambda n: pl.BlockSpec((tm, n), lambda i: (i, 0))
    return pl.pallas_call(
        _out_body,
        grid=(T // tm,),
        in_specs=[rows(D_MODEL), rows(PLE_DIM), rows(512), rows(1024), rows(3072)]
                 + [full(w) for w in weights] + [full(gn) for gn in gains],
        out_specs=rows(D_MODEL),
        out_shape=jax.ShapeDtypeStruct((T, D_MODEL), F32),
        scratch_shapes=[pltpu.VMEM((tm, D_FF), BF16)],
        compiler_params=pltpu.CompilerParams(
            dimension_semantics=("arbitrary",), vmem_limit_bytes=VMEM_LIMIT),
        name="merge_ffn",
    )(x2, p2, oa, og, act, *weights, *gains)


def _t5_bucket_table():
    q_loc = np.arange(BLOCK)[:, None] + BLOCK
    k_loc = np.arange(2 * BLOCK)[None, :]
    dist = np.maximum(q_loc - k_loc, 0)
    max_exact = NUM_BUCKETS // 2
    d_f = np.maximum(dist, max_exact).astype(np.float32)
    large = max_exact + (np.log(d_f / max_exact) / math.log(MAX_DISTANCE / max_exact)
                         * (NUM_BUCKETS - max_exact)).astype(np.int32)
    large = np.minimum(large, NUM_BUCKETS - 1)
    bucket = np.where(dist < max_exact, dist, large)
    raw = q_loc - k_loc
    in_window = (raw >= 0) & (raw < WINDOW)
    return bucket.astype(np.int32), in_window


def _level_table():
    r = np.arange(GLA_CHUNK)[:, None]
    c = np.arange(GLA_CHUNK)[None, :]
    x = np.bitwise_xor(r, c)
    lv = np.where(x > 0, np.floor(np.log2(np.maximum(x, 1))).astype(np.int32), GLA_LEVELS)
    return np.where(r >= c, lv, -1).astype(np.int32)


def _proj_col_scale():
    s = np.ones((N_MAIN,), np.float32)
    s[0:ATTN_HEADS * HEAD_DIM] = HEAD_DIM ** -0.5
    s[SEG_QKG[0]:SEG_QKG[0] + GLA_KEY_DIM] = GLA_HEAD_K ** -0.5
    return s


def _attn_bias(rel_table, sinks):
    bucket, in_window = _t5_bucket_table()
    bucket = jnp.asarray(bucket)
    rel = rel_table.astype(F32)
    bias = jnp.zeros((ATTN_HEADS, BLOCK, 2 * BLOCK), F32)
    for b in range(NUM_BUCKETS):
        bias = jnp.where(bucket[None] == b, rel[b][:, None, None], bias)
    bias = jnp.where(jnp.asarray(in_window)[None], bias, NEG)
    col0 = jnp.asarray(np.arange(2 * BLOCK) == 0)[None, None, :]
    return jnp.where(col0, sinks.astype(F32)[:, None, None], bias)


def kernel(x, p, w_in, w_gk2, b_gk, sinks, rel_table, w_proj_attn, w_proj_gla, gla_norm, w_out,
           norm_mix, norm_ffn, w_ffn_in, w_ffn_out, norm_ple, w_ple_gate, w_ple, norm_final):
    B, S, D = x.shape
    T = B * S
    assert w_in.shape[0] == 1, "single-layer trunk only"
    i = 0
    lv = jnp.asarray(_level_table())
    tri = jnp.asarray(np.tril(np.ones((GLA_CHUNK, GLA_CHUNK), np.float32)), dtype=BF16)
    bias = _attn_bias(rel_table, sinks[i])

    xs = x.reshape(T, D)
    w = w_in[i]
    gk0 = SEG_ACT[0]
    w_main = (jnp.concatenate([w[:, :gk0], w[:, gk0 + GLA_GATE_RANK:]], axis=1)
              * jnp.asarray(_proj_col_scale())[None, :]).astype(BF16)
    w_gk = jnp.pad(w[:, gk0:gk0 + GLA_GATE_RANK], ((0, 0), (0, LANES - GLA_GATE_RANK))).astype(BF16)
    w_gk2p = jnp.pad(w_gk2[i], ((0, LANES - GLA_GATE_RANK), (0, 0))).astype(BF16)

    qkva, qkg, vg, g, act = _proj_call(
        xs, norm_mix[i].reshape(1, D), w_main, w_gk, w_gk2p, b_gk[i].reshape(1, -1), tri)

    oa = _attn_call(qkva.reshape(B, S, 768), bias).reshape(T, 512)
    og = _gla_call(qkg.reshape(B, S, 1024), vg.reshape(B, S, 1024), g.reshape(B, S, 512), lv).reshape(T, 1024)

    weights = [
        w_proj_attn[i].astype(BF16), w_proj_gla[i].astype(BF16), w_out[i].astype(BF16),
        w_ffn_in[i].astype(BF16), w_ffn_out[i].astype(BF16), w_ple_gate[i].astype(BF16), w_ple[i].astype(BF16),
    ]
    gains = [gla_norm[i].reshape(1, -1), norm_ffn[i].reshape(1, D), norm_ple[i].reshape(1, D),
             norm_final.reshape(1, D)]
    out = _out_call(xs, p[i].reshape(T, PLE_DIM), oa, og, act, weights, gains)
    return out.reshape(B, S, D)
```

```python
import math

import numpy as np
import jax
import jax.numpy as jnp
from jax import lax
from jax.experimental import pallas as pl
from jax.experimental.pallas import tpu as pltpu

F32 = jnp.float32
BF16 = jnp.bfloat16

D_MODEL = 1024
ATTN_HEADS = 8
ATTN_KV_HEADS = 2
ATTN_GROUP = ATTN_HEADS // ATTN_KV_HEADS
HEAD_DIM = 64
WINDOW = 128
BLOCK = 128
NUM_BUCKETS = 32
MAX_DISTANCE = 128
GLA_HEADS = 4
GLA_KEY_DIM = D_MODEL // 2
GLA_VAL_DIM = D_MODEL
GLA_HEAD_K = GLA_KEY_DIM // GLA_HEADS
GLA_HEAD_V = GLA_VAL_DIM // GLA_HEADS
GLA_GATE_RANK = 16
GLA_GATE_TAU = 16.0
D_FF = -(-8 * D_MODEL // (3 * 256)) * 256
PLE_DIM = 256
EPS = 1e-6
NEG = -1e30
LOG2E = math.log2(math.e)

LANES = 128
SUBLANES = 8

TM_PROJ = 512
GLA_CHUNK = 128
GLA_LEVELS = 7
GLA_BATCH_BLOCK = 4
ATTN_Q_ROWS = 512
TM_OUT = 256
FFN_COLS = 256
VMEM_LIMIT = 56 * 1024 * 1024

N_QKVA = 768
SEG_QKG = (768, 1792)
SEG_VG = (1792, 2816)
GK0 = 2816
N_FRONT = 2816
N_ACT = 3072

NT = (((1,), (1,)), ((), ()))


def _sigmoid(x):
    return 0.5 * jnp.tanh(0.5 * x) + 0.5


def _rms(x, gain):
    ms = jnp.mean(x * x, axis=-1, keepdims=True)
    return x * lax.rsqrt(ms + EPS) * gain


def _chunk_cumsum(la):
    tm, n = la.shape
    groups = tm // SUBLANES
    x = la.reshape(groups, SUBLANES, n)
    sub = lax.broadcasted_iota(jnp.int32, x.shape, 1)
    for s in (1, 2, 4):
        x = x + jnp.where(sub >= s, pltpu.roll(x, s, 1), 0.0)
    per_chunk = GLA_CHUNK // SUBLANES
    out = []
    carry = None
    for i in range(groups):
        blk = x[i]
        if i % per_chunk:
            blk = blk + carry
        out.append(blk)
        carry = jnp.broadcast_to(blk[SUBLANES - 1:SUBLANES, :], (SUBLANES, n))
    return jnp.concatenate(out, axis=0)


def _proj_body(x_ref, nm_ref, wf_ref, wa_ref, wgk_ref, wgk2_ref, bgk_ref,
               qkva_ref, qkg_ref, vg_ref, g_ref, act_ref):
    h = _rms(x_ref[...], nm_ref[...]).astype(BF16)

    def proj(w_ref, c0, c1):
        return jnp.dot(h, w_ref[:, c0:c1], preferred_element_type=F32)

    qkva_ref[:, 0:512] = proj(wf_ref, 0, 512).astype(BF16)
    qkva_ref[:, 512:768] = proj(wf_ref, 512, 768).astype(BF16)
    for c in range(2):
        a = SEG_QKG[0] + 512 * c
        qkg_ref[:, 512 * c:512 * (c + 1)] = proj(wf_ref, a, a + 512).astype(BF16)
    for c in range(2):
        a = SEG_VG[0] + 512 * c
        vg_ref[:, 512 * c:512 * (c + 1)] = proj(wf_ref, a, a + 512).astype(BF16)
    for c in range(N_ACT // 512):
        r = proj(wa_ref, 512 * c, 512 * (c + 1))
        s = _sigmoid(r)
        if c < 2:
            s = r * s
        act_ref[:, 512 * c:512 * (c + 1)] = s.astype(BF16)

    gk = jnp.dot(h, wgk_ref[...], preferred_element_type=F32).astype(BF16)
    z = jnp.dot(gk, wgk2_ref[...], preferred_element_type=F32) + bgk_ref[...]
    la = (jnp.minimum(z, 0.0) - jnp.log(1.0 + jnp.exp(-jnp.abs(z)))) * (LOG2E / GLA_GATE_TAU)
    g_ref[...] = _chunk_cumsum(la)


def _proj_call(x2, nm, w_front, w_act, w_gk, w_gk2, b_gk):
    T = x2.shape[0]
    tm = TM_PROJ
    full = lambda a: pl.BlockSpec(a.shape, lambda i: (0,) * a.ndim)
    rows = lambda n: pl.BlockSpec((tm, n), lambda i: (i, 0))
    return pl.pallas_call(
        _proj_body,
        grid=(T // tm,),
        in_specs=[rows(D_MODEL), full(nm), full(w_front), full(w_act), full(w_gk), full(w_gk2), full(b_gk)],
        out_specs=[rows(N_QKVA), rows(1024), rows(1024), rows(512), rows(N_ACT)],
        out_shape=[
            jax.ShapeDtypeStruct((T, N_QKVA), BF16),
            jax.ShapeDtypeStruct((T, 1024), BF16),
            jax.ShapeDtypeStruct((T, 1024), BF16),
            jax.ShapeDtypeStruct((T, 512), F32),
            jax.ShapeDtypeStruct((T, N_ACT), BF16),
        ],
        compiler_params=pltpu.CompilerParams(
            dimension_semantics=("arbitrary",), vmem_limit_bytes=VMEM_LIMIT),
        name="in_proj",
    )(x2, nm, w_front, w_act, w_gk, w_gk2, b_gk)


def _attn_body(q_ref, kp_ref, kc_ref, vp_ref, vc_ref, bias_ref, o_ref):
    first = pl.program_id(1) == 0
    nq = ATTN_Q_ROWS // BLOCK
    band_row = lax.broadcasted_iota(jnp.int32, (2 * BLOCK, LANES), 0)
    low = lax.broadcasted_iota(jnp.int32, (2 * BLOCK, LANES), 1) < HEAD_DIM
    col = lax.broadcasted_iota(jnp.int32, (BLOCK, 2 * BLOCK), 1)
    no_prev = jnp.logical_and(first, jnp.logical_and(col >= 1, col < BLOCK))
    low_o = lax.broadcasted_iota(jnp.int32, (BLOCK, LANES), 1) < HEAD_DIM
    ones = jnp.ones((2 * BLOCK, LANES), BF16)
    for j in range(nq):
        r0 = j * BLOCK
        if j == 0:
            kband = jnp.concatenate([kp_ref[0], kc_ref[0, 0:BLOCK, :]], axis=0)
            vband = jnp.concatenate([vp_ref[0], vc_ref[0, 0:BLOCK, :]], axis=0)
        else:
            kband = kc_ref[0, r0 - BLOCK:r0 + BLOCK, :]
            vband = vc_ref[0, r0 - BLOCK:r0 + BLOCK, :]
        zero = jnp.zeros_like(kband)
        kband = jnp.where(band_row == 0, zero, kband)
        vband = jnp.where(band_row == 0, zero, vband)
        kroll = pltpu.roll(kband, HEAD_DIM, 1)
        vroll = pltpu.roll(vband, HEAD_DIM, 1)
        ktile = ((jnp.where(low, kband, zero), jnp.where(low, zero, kroll)),
                 (jnp.where(low, kroll, zero), jnp.where(low, zero, kband)))
        probs = [[None, None], [None, None]]
        for kv in range(ATTN_KV_HEADS):
            qg = q_ref[0, r0:r0 + BLOCK, 2 * kv * LANES:(2 * kv + 2) * LANES]
            qg = jnp.concatenate([qg[:, :LANES], qg[:, LANES:]], axis=0)
            for odd in range(2):
                s2 = lax.dot_general(qg, ktile[kv][odd], NT, preferred_element_type=F32)
                ps = []
                for half in range(2):
                    head = 2 * (2 * kv + half) + odd
                    s = s2[half * BLOCK:(half + 1) * BLOCK] + bias_ref[head]
                    if j == 0:
                        s = jnp.where(no_prev, NEG, s)
                    ps.append(jnp.exp(s - jnp.max(s, axis=-1, keepdims=True)).astype(BF16))
                probs[kv][odd] = jnp.concatenate(ps, axis=0)
        vnat = jnp.concatenate([vband, ones], axis=1)
        vrol = jnp.concatenate([vroll, ones], axis=1)
        od_nat = jnp.dot(jnp.concatenate([probs[0][0], probs[1][1]], axis=0), vnat, preferred_element_type=F32)
        od_rol = jnp.dot(jnp.concatenate([probs[0][1], probs[1][0]], axis=0), vrol, preferred_element_type=F32)
        for blk in range(ATTN_HEADS // 2):
            rows = slice(blk * BLOCK, (blk + 1) * BLOCK)
            o_nat = od_nat[rows, :LANES] / od_nat[rows, LANES:]
            o_rol = od_rol[rows, :LANES] / od_rol[rows, LANES:]
            even, odd_ = (o_nat, o_rol) if blk < ATTN_HEADS // 4 else (o_rol, o_nat)
            o_ref[0, r0:r0 + BLOCK, blk * LANES:(blk + 1) * LANES] = jnp.where(low_o, even, odd_).astype(BF16)


def _attn_call(qkva, bias):
    B, S, _ = qkva.shape
    tq = ATTN_Q_ROWS
    nprev = tq // BLOCK
    return pl.pallas_call(
        _attn_body,
        grid=(B, S // tq),
        in_specs=[
            pl.BlockSpec((1, tq, 512), lambda b, n: (b, n, 0)),
            pl.BlockSpec((1, BLOCK, LANES), lambda b, n: (b, jnp.maximum(n * nprev - 1, 0), 4)),
            pl.BlockSpec((1, tq, LANES), lambda b, n: (b, n, 4)),
            pl.BlockSpec((1, BLOCK, LANES), lambda b, n: (b, jnp.maximum(n * nprev - 1, 0), 5)),
            pl.BlockSpec((1, tq, LANES), lambda b, n: (b, n, 5)),
            pl.BlockSpec(bias.shape, lambda b, n: (0, 0, 0)),
        ],
        out_specs=pl.BlockSpec((1, tq, 512), lambda b, n: (b, n, 0)),
        out_shape=jax.ShapeDtypeStruct((B, S, 512), BF16),
        compiler_params=pltpu.CompilerParams(
            dimension_semantics=("arbitrary", "arbitrary"), vmem_limit_bytes=VMEM_LIMIT),
        name="swa_attn",
    )(qkva, qkva, qkva, qkva, qkva, bias)


def _anchor_small(grow, b):
    C = GLA_CHUNK

    def row(r):
        return jnp.broadcast_to(grow(r), (SUBLANES, LANES))

    sub = lax.broadcasted_iota(jnp.int32, (SUBLANES, LANES), 0)
    parts = []
    for i in range(C // SUBLANES):
        r0 = i * SUBLANES
        if b == 4:
            a = row(r0 + 3)
        elif b == 2:
            a = jnp.where(sub < 4, row(r0 + 1), row(r0 + 5))
        else:
            a = jnp.where(sub < 2, row(r0), jnp.where(sub < 4, row(r0 + 2),
                                                      jnp.where(sub < 6, row(r0 + 4), row(r0 + 6))))
        parts.append(a)
    return jnp.concatenate(parts, axis=0)


def _level_operand(q, k, g, grow, b, row):
    C = GLA_CHUNK
    if b >= SUBLANES:
        expo, src = [], []
        for p in range(C // (2 * b)):
            lo, mid, hi = 2 * b * p, 2 * b * p + b, 2 * b * (p + 1)
            a = jnp.broadcast_to(grow(mid - 1), (b, LANES))
            expo += [a - g[lo:mid], g[mid:hi] - a]
            src += [k[lo:mid], q[mid:hi]]
        expo = jnp.concatenate(expo, axis=0)
        src = jnp.concatenate(src, axis=0)
    else:
        upper = (row & b) != 0
        d = g - _anchor_small(grow, b)
        expo = jnp.where(upper, d, -d)
        src = jnp.where(upper, q, k)
    return (src * jnp.exp2(expo)).astype(BF16)


def _gla_body(qk_ref, v_ref, g_ref, lv_ref, o_ref, s_scr):
    C = GLA_CHUNK

    @pl.when(pl.program_id(1) == 0)
    def _():
        s_scr[...] = jnp.zeros_like(s_scr)

    lv = lv_ref[...]
    row = lax.broadcasted_iota(jnp.int32, (C, LANES), 0)
    for bi in range(qk_ref.shape[0]):
        for h in range(GLA_HEADS):
            cols = slice(h * GLA_HEAD_K, (h + 1) * GLA_HEAD_K)
            grow = lambda r, bi=bi, cols=cols: g_ref[bi, r:r + 1, cols]
            q_b = qk_ref[bi, :, cols]
            k_b = qk_ref[bi, :, GLA_KEY_DIM + h * GLA_HEAD_K:GLA_KEY_DIM + (h + 1) * GLA_HEAD_K]
            q = q_b.astype(F32)
            k = k_b.astype(F32)
            v = v_ref[bi, :, h * GLA_HEAD_V:(h + 1) * GLA_HEAD_V]
            g = g_ref[bi, :, cols]
            glast = jnp.broadcast_to(grow(C - 1), (C, GLA_HEAD_K))

            scores = jnp.where(lv == GLA_LEVELS, lax.dot_general(q_b, k_b, NT, preferred_element_type=F32), 0.0)
            for l in range(GLA_LEVELS):
                r = _level_operand(q, k, g, grow, 1 << l, row)
                scores = jnp.where(lv == l, lax.dot_general(r, r, NT, preferred_element_type=F32), scores)

            si = bi * GLA_HEADS + h
            state = s_scr[si]
            qt = (q * jnp.exp2(g)).astype(BF16)
            lhs = jnp.concatenate([qt, scores.astype(BF16)], axis=1)
            rhs = jnp.concatenate([state.astype(BF16), v], axis=0)
            o = jnp.dot(lhs, rhs, preferred_element_type=F32)
            o_ref[bi, :, h * GLA_HEAD_V:(h + 1) * GLA_HEAD_V] = o.astype(BF16)

            kd_t = (k * jnp.exp2(glast - g)).T.astype(BF16)
            gam = jnp.exp2(glast.T)
            upd = jnp.dot(kd_t, v, preferred_element_type=F32)
            s_scr[si] = state * jnp.concatenate([gam, gam], axis=1) + upd


def _gla_call(qkg, vg, g, lv):
    B, S, _ = qkg.shape
    C = GLA_CHUNK
    nb = GLA_BATCH_BLOCK
    return pl.pallas_call(
        _gla_body,
        grid=(B // nb, S // C),
        in_specs=[
            pl.BlockSpec((nb, C, 1024), lambda b, c: (b, c, 0)),
            pl.BlockSpec((nb, C, 1024), lambda b, c: (b, c, 0)),
            pl.BlockSpec((nb, C, 512), lambda b, c: (b, c, 0)),
            pl.BlockSpec((C, C), lambda b, c: (0, 0)),
        ],
        out_specs=pl.BlockSpec((nb, C, 1024), lambda b, c: (b, c, 0)),
        out_shape=jax.ShapeDtypeStruct((B, S, 1024), BF16),
        scratch_shapes=[pltpu.VMEM((nb * GLA_HEADS, GLA_HEAD_K, GLA_HEAD_V), F32)],
        compiler_params=pltpu.CompilerParams(
            dimension_semantics=("arbitrary", "arbitrary"), vmem_limit_bytes=VMEM_LIMIT),
        name="gla",
    )(qkg, vg, g, lv)


def _out_body(x_ref, p_ref, oa_ref, og_ref, act_ref,
              wpa_ref, wpg_ref, wo_ref, wfi_ref, wfo_ref, wpgate_ref, wple_ref,
              gn_ref, nffn_ref, nple_ref, nfin_ref, out_ref, a_scr):
    ya = jnp.dot(oa_ref[...], wpa_ref[...], preferred_element_type=F32)
    parts = []
    for h in range(GLA_HEADS):
        cols = slice(h * GLA_HEAD_V, (h + 1) * GLA_HEAD_V)
        oh = _rms(og_ref[:, cols].astype(F32), gn_ref[...]) * act_ref[:, cols].astype(F32)
        parts.append(oh.astype(BF16))
    yg = jnp.dot(jnp.concatenate(parts, axis=1), wpg_ref[...], preferred_element_type=F32)
    mixed = (act_ref[:, 1024:2048].astype(F32) * ya + act_ref[:, 2048:3072].astype(F32) * yg)
    x1 = x_ref[...] + jnp.dot(mixed.astype(BF16), wo_ref[...], preferred_element_type=F32)

    h2 = _rms(x1, nffn_ref[...]).astype(BF16)
    w = FFN_COLS
    for c in range(D_FF // w):
        w_gu = jnp.concatenate(
            [wfi_ref[:, w * c:w * (c + 1)], wfi_ref[:, D_FF + w * c:D_FF + w * (c + 1)]], axis=1)
        gu = jnp.dot(h2, w_gu, preferred_element_type=F32)
        gg = gu[:, :w]
        a_scr[:, w * c:w * (c + 1)] = (gg * _sigmoid(gg) * gu[:, w:]).astype(BF16)
    x2 = x1 + jnp.dot(a_scr[...], wfo_ref[...], preferred_element_type=F32)

    h3 = _rms(x2, nple_ref[...]).astype(BF16)
    gate = _sigmoid(jnp.dot(h3, wpgate_ref[...], preferred_element_type=F32))
    pe = jnp.dot(p_ref[...].astype(BF16), wple_ref[...], preferred_element_type=F32)
    x3 = x2 + gate * pe
    out_ref[...] = _rms(x3, nfin_ref[...])


def _out_call(x2, p2, oa, og, act, weights, gains):
    T = x2.shape[0]
    tm = TM_OUT
    full = lambda a: pl.BlockSpec(a.shape, lambda i: (0,) * a.ndim)
    rows = lambda n: pl.BlockSpec((tm, n), lambda i: (i, 0))
    return pl.pallas_call(
        _out_body,
        grid=(T // tm,),
        in_specs=[rows(D_MODEL), rows(PLE_DIM), rows(512), rows(1024), rows(N_ACT)]
                 + [full(w) for w in weights] + [full(gn) for gn in gains],
        out_specs=rows(D_MODEL),
        out_shape=jax.ShapeDtypeStruct((T, D_MODEL), F32),
        scratch_shapes=[pltpu.VMEM((tm, D_FF), BF16)],
        compiler_params=pltpu.CompilerParams(
            dimension_semantics=("arbitrary",), vmem_limit_bytes=VMEM_LIMIT),
        name="merge_ffn",
    )(x2, p2, oa, og, act, *weights, *gains)


def _t5_bucket_table():
    q_loc = np.arange(BLOCK)[:, None] + BLOCK
    k_loc = np.arange(2 * BLOCK)[None, :]
    dist = np.maximum(q_loc - k_loc, 0)
    max_exact = NUM_BUCKETS // 2
    d_f = np.maximum(dist, max_exact).astype(np.float32)
    large = max_exact + (np.log(d_f / max_exact) / math.log(MAX_DISTANCE / max_exact)
                         * (NUM_BUCKETS - max_exact)).astype(np.int32)
    large = np.minimum(large, NUM_BUCKETS - 1)
    bucket = np.where(dist < max_exact, dist, large)
    raw = q_loc - k_loc
    in_window = (raw >= 0) & (raw < WINDOW)
    return bucket.astype(np.int32), in_window


def _level_table():
    r = np.arange(GLA_CHUNK)[:, None]
    c = np.arange(GLA_CHUNK)[None, :]
    x = np.bitwise_xor(r, c)
    lv = np.where(x > 0, np.floor(np.log2(np.maximum(x, 1))).astype(np.int32), GLA_LEVELS)
    return np.where(r >= c, lv, -1).astype(np.int32)


def _front_col_scale():
    s = np.ones((N_FRONT,), np.float32)
    s[0:ATTN_HEADS * HEAD_DIM] = HEAD_DIM ** -0.5
    s[SEG_QKG[0]:SEG_QKG[0] + GLA_KEY_DIM] = GLA_HEAD_K ** -0.5
    return s


def _attn_bias(rel_table, sinks):
    bucket, in_window = _t5_bucket_table()
    bucket = jnp.asarray(bucket)
    rel = rel_table.astype(F32)
    bias = jnp.zeros((ATTN_HEADS, BLOCK, 2 * BLOCK), F32)
    for b in range(NUM_BUCKETS):
        bias = jnp.where(bucket[None] == b, rel[b][:, None, None], bias)
    bias = jnp.where(jnp.asarray(in_window)[None], bias, NEG)
    col0 = jnp.asarray(np.arange(2 * BLOCK) == 0)[None, None, :]
    return jnp.where(col0, sinks.astype(F32)[:, None, None], bias)


def kernel(x, p, w_in, w_gk2, b_gk, sinks, rel_table, w_proj_attn, w_proj_gla, gla_norm, w_out,
           norm_mix, norm_ffn, w_ffn_in, w_ffn_out, norm_ple, w_ple_gate, w_ple, norm_final):
    B, S, D = x.shape
    T = B * S
    assert w_in.shape[0] == 1, "single-layer trunk only"
    i = 0
    lv = jnp.asarray(_level_table())
    bias = _attn_bias(rel_table, sinks[i])

    xs = x.reshape(T, D)
    w = w_in[i]
    w_front = (w[:, :N_FRONT] * jnp.asarray(_front_col_scale())[None, :]).astype(BF16)
    w_act = w[:, GK0 + GLA_GATE_RANK:].astype(BF16)
    w_gk = jnp.pad(w[:, GK0:GK0 + GLA_GATE_RANK], ((0, 0), (0, LANES - GLA_GATE_RANK))).astype(BF16)
    w_gk2p = jnp.pad(w_gk2[i], ((0, LANES - GLA_GATE_RANK), (0, 0))).astype(BF16)

    qkva, qkg, vg, g, act = _proj_call(
        xs, norm_mix[i].reshape(1, D), w_front, w_act, w_gk, w_gk2p, b_gk[i].reshape(1, -1))

    oa = _attn_call(qkva.reshape(B, S, N_QKVA), bias).reshape(T, 512)
    og = _gla_call(qkg.reshape(B, S, 1024), vg.reshape(B, S, 1024), g.reshape(B, S, 512), lv).reshape(T, 1024)

    weights = [
        w_proj_attn[i].astype(BF16), w_proj_gla[i].astype(BF16), w_out[i].astype(BF16),
        w_ffn_in[i].astype(BF16), w_ffn_out[i].astype(BF16), w_ple_gate[i].astype(BF16), w_ple[i].astype(BF16),
    ]
    gains = [gla_norm[i].reshape(1, -1), norm_ffn[i].reshape(1, D), norm_ple[i].reshape(1, D),
             norm_final.reshape(1, D)]
    out = _out_call(xs, p[i].reshape(T, PLE_DIM), oa, og, act, weights, gains)
    return out.reshape(B, S, D)
```

```python
import math
from functools import partial

import numpy as np
import jax
import jax.numpy as jnp
from jax import lax
from jax.experimental import pallas as pl
from jax.experimental.pallas import tpu as pltpu

F32 = jnp.float32
BF16 = jnp.bfloat16

D_MODEL = 1024
ATTN_HEADS = 8
ATTN_KV_HEADS = 2
ATTN_GROUP = ATTN_HEADS // ATTN_KV_HEADS
HEAD_DIM = 64
WINDOW = 128
BLOCK = 128
NUM_BUCKETS = 32
MAX_DISTANCE = 128
GLA_HEADS = 4
GLA_KEY_DIM = D_MODEL // 2
GLA_VAL_DIM = D_MODEL
GLA_HEAD_K = GLA_KEY_DIM // GLA_HEADS
GLA_HEAD_V = GLA_VAL_DIM // GLA_HEADS
GLA_GATE_RANK = 16
GLA_GATE_TAU = 16.0
D_FF = -(-8 * D_MODEL // (3 * 256)) * 256
PLE_DIM = 256
EPS = 1e-6
NEG = -1e30
LOG2E = math.log2(math.e)

LANES = 128
SUBLANES = 8

TM_MIX = 512
GLA_CHUNK = 128
GLA_LEVELS = 7
TM_OUT = 256
FFN_COLS = 256
VMEM_LIMIT = 56 * 1024 * 1024

N_QA = ATTN_HEADS * HEAD_DIM
N_QKVA = 768
SEG_QKG = (768, 1792)
SEG_VG = (1792, 2816)
GK0 = 2816
N_FRONT = 2816
N_ACT = 3072

NT = (((1,), (1,)), ((), ()))


def _sigmoid(x):
    return 0.5 * jnp.tanh(0.5 * x) + 0.5


def _rms(x, gain):
    ms = jnp.mean(x * x, axis=-1, keepdims=True)
    return x * lax.rsqrt(ms + EPS) * gain


def _chunk_cumsum(la):
    tm, n = la.shape
    groups = tm // SUBLANES
    x = la.reshape(groups, SUBLANES, n)
    sub = lax.broadcasted_iota(jnp.int32, x.shape, 1)
    for s in (1, 2, 4):
        x = x + jnp.where(sub >= s, pltpu.roll(x, s, 1), 0.0)
    per_chunk = GLA_CHUNK // SUBLANES
    out = []
    carry = None
    for i in range(groups):
        blk = x[i]
        if i % per_chunk:
            blk = blk + carry
        out.append(blk)
        carry = jnp.broadcast_to(blk[SUBLANES - 1:SUBLANES, :], (SUBLANES, n))
    return jnp.concatenate(out, axis=0)


def _attn_probs(j, first, q_scr, kv_scr, bias_ref):
    r0 = j * BLOCK
    band_row = lax.broadcasted_iota(jnp.int32, (2 * BLOCK, LANES), 0)
    low = lax.broadcasted_iota(jnp.int32, (2 * BLOCK, LANES), 1) < HEAD_DIM
    ones = jnp.ones((2 * BLOCK, LANES), BF16)
    kband = kv_scr[r0:r0 + 2 * BLOCK, 0:LANES]
    vband = kv_scr[r0:r0 + 2 * BLOCK, LANES:2 * LANES]
    zero = jnp.zeros_like(kband)
    kband = jnp.where(band_row == 0, zero, kband)
    vband = jnp.where(band_row == 0, zero, vband)
    kroll = pltpu.roll(kband, HEAD_DIM, 1)
    vroll = pltpu.roll(vband, HEAD_DIM, 1)
    ktile = ((jnp.where(low, kband, zero), jnp.where(low, zero, kroll)),
             (jnp.where(low, kroll, zero), jnp.where(low, zero, kband)))
    if j == 0:
        col = lax.broadcasted_iota(jnp.int32, (BLOCK, 2 * BLOCK), 1)
        no_prev = jnp.logical_and(first, jnp.logical_and(col >= 1, col < BLOCK))
    probs = [[None, None], [None, None]]
    for kv in range(ATTN_KV_HEADS):
        qg = q_scr[r0:r0 + BLOCK, 2 * kv * LANES:(2 * kv + 2) * LANES]
        qg = jnp.concatenate([qg[:, :LANES], qg[:, LANES:]], axis=0)
        for odd in range(2):
            s2 = lax.dot_general(qg, ktile[kv][odd], NT, preferred_element_type=F32)
            ps = []
            for half in range(2):
                head = 2 * (2 * kv + half) + odd
                s = s2[half * BLOCK:(half + 1) * BLOCK] + bias_ref[head]
                if j == 0:
                    s = jnp.where(no_prev, NEG, s)
                ps.append(jnp.exp(s - jnp.max(s, axis=-1, keepdims=True)).astype(BF16))
            probs[kv][odd] = jnp.concatenate(ps, axis=0)
    vnat = jnp.concatenate([vband, ones], axis=1)
    vrol = jnp.concatenate([vroll, ones], axis=1)
    p_nat = jnp.concatenate([probs[0][0], probs[1][1]], axis=0)
    p_rol = jnp.concatenate([probs[0][1], probs[1][0]], axis=0)
    return p_nat, vnat, p_rol, vrol


def _attn_out(j, p_nat, vnat, p_rol, vrol, oa_ref):
    r0 = j * BLOCK
    low_o = lax.broadcasted_iota(jnp.int32, (BLOCK, LANES), 1) < HEAD_DIM
    od_nat = jnp.dot(p_nat, vnat, preferred_element_type=F32)
    od_rol = jnp.dot(p_rol, vrol, preferred_element_type=F32)
    for blk in range(ATTN_HEADS // 2):
        rows = slice(blk * BLOCK, (blk + 1) * BLOCK)
        o_nat = od_nat[rows, :LANES] / od_nat[rows, LANES:]
        o_rol = od_rol[rows, :LANES] / od_rol[rows, LANES:]
        even, odd_ = (o_nat, o_rol) if blk < ATTN_HEADS // 4 else (o_rol, o_nat)
        oa_ref[r0:r0 + BLOCK, blk * LANES:(blk + 1) * LANES] = jnp.where(low_o, even, odd_).astype(BF16)


def _anchor_small(grow, b):
    C = GLA_CHUNK

    def row(r):
        return jnp.broadcast_to(grow(r), (SUBLANES, LANES))

    sub = lax.broadcasted_iota(jnp.int32, (SUBLANES, LANES), 0)
    parts = []
    for i in range(C // SUBLANES):
        r0 = i * SUBLANES
        if b == 4:
            a = row(r0 + 3)
        elif b == 2:
            a = jnp.where(sub < 4, row(r0 + 1), row(r0 + 5))
        else:
            a = jnp.where(sub < 2, row(r0), jnp.where(sub < 4, row(r0 + 2),
                                                      jnp.where(sub < 6, row(r0 + 4), row(r0 + 6))))
        parts.append(a)
    return jnp.concatenate(parts, axis=0)


def _level_operand(q, k, g, grow, b, row):
    C = GLA_CHUNK
    if b >= SUBLANES:
        expo, src = [], []
        for p in range(C // (2 * b)):
            lo, mid, hi = 2 * b * p, 2 * b * p + b, 2 * b * (p + 1)
            a = jnp.broadcast_to(grow(mid - 1), (b, LANES))
            expo += [a - g[lo:mid], g[mid:hi] - a]
            src += [k[lo:mid], q[mid:hi]]
        expo = jnp.concatenate(expo, axis=0)
        src = jnp.concatenate(src, axis=0)
    else:
        upper = (row & b) != 0
        d = g - _anchor_small(grow, b)
        expo = jnp.where(upper, d, -d)
        src = jnp.where(upper, q, k)
    return (src * jnp.exp2(expo)).astype(BF16)


def _gla_scores(c, qkg_scr, vg_scr, g_scr, lv):
    C = GLA_CHUNK
    r0 = c * C
    row = lax.broadcasted_iota(jnp.int32, (C, LANES), 0)
    per_head = []
    for h in range(GLA_HEADS):
        cols = slice(h * GLA_HEAD_K, (h + 1) * GLA_HEAD_K)
        grow = lambda r, cols=cols: g_scr[r0 + r:r0 + r + 1, cols]
        q_b = qkg_scr[r0:r0 + C, cols]
        k_b = qkg_scr[r0:r0 + C, GLA_KEY_DIM + h * GLA_HEAD_K:GLA_KEY_DIM + (h + 1) * GLA_HEAD_K]
        q = q_b.astype(F32)
        k = k_b.astype(F32)
        v = vg_scr[r0:r0 + C, h * GLA_HEAD_V:(h + 1) * GLA_HEAD_V]
        g = g_scr[r0:r0 + C, cols]
        glast = jnp.broadcast_to(grow(C - 1), (C, GLA_HEAD_K))

        scores = jnp.where(lv == GLA_LEVELS, lax.dot_general(q_b, k_b, NT, preferred_element_type=F32), 0.0)
        for l in range(GLA_LEVELS):
            r = _level_operand(q, k, g, grow, 1 << l, row)
            scores = jnp.where(lv == l, lax.dot_general(r, r, NT, preferred_element_type=F32), scores)
        per_head.append((q, k, v, g, glast, scores))
    return per_head


def _gla_state(c, per_head, s_scr, og_ref):
    C = GLA_CHUNK
    r0 = c * C
    for h, (q, k, v, g, glast, scores) in enumerate(per_head):
        state = s_scr[h]
        qt = (q * jnp.exp2(g)).astype(BF16)
        lhs = jnp.concatenate([qt, scores.astype(BF16)], axis=1)
        rhs = jnp.concatenate([state.astype(BF16), v], axis=0)
        o = jnp.dot(lhs, rhs, preferred_element_type=F32)
        og_ref[r0:r0 + C, h * GLA_HEAD_V:(h + 1) * GLA_HEAD_V] = o.astype(BF16)

        kd_t = (k * jnp.exp2(glast - g)).T.astype(BF16)
        gam = jnp.exp2(glast.T)
        upd = jnp.dot(kd_t, v, preferred_element_type=F32)
        s_scr[h] = state * jnp.concatenate([gam, gam], axis=1) + upd


def _mixer_body(tiles_per_seq, x_ref, nm_ref, wf_ref, wa_ref, wgk_ref, wgk2_ref, bgk_ref, bias_ref, lv_ref,
                oa_ref, og_ref, act_ref, q_scr, kv_scr, qkg_scr, vg_scr, g_scr, s_scr):
    i = pl.program_id(0)
    first = (i % tiles_per_seq) == 0

    @pl.when(i == 0)
    def _():
        kv_scr[0:BLOCK, :] = jnp.zeros((BLOCK, 2 * LANES), BF16)

    @pl.when(first)
    def _():
        s_scr[...] = jnp.zeros_like(s_scr)

    h = _rms(x_ref[...], nm_ref[...]).astype(BF16)

    def proj(w_ref, c0, c1):
        return jnp.dot(h, w_ref[:, c0:c1], preferred_element_type=F32)

    w_kvgk = jnp.concatenate(
        [wf_ref[:, N_QA:N_QKVA], wgk_ref[...], jnp.zeros((D_MODEL, LANES), BF16)], axis=1)
    kvgk = jnp.dot(h, w_kvgk, preferred_element_type=F32)
    kv_scr[BLOCK:BLOCK + TM_MIX, :] = kvgk[:, :2 * LANES].astype(BF16)
    gk = kvgk[:, 2 * LANES:3 * LANES].astype(BF16)

    def decay_chunk(j):
        rows = slice(j * GLA_CHUNK, (j + 1) * GLA_CHUNK)
        z = jnp.dot(gk[rows], wgk2_ref[...], preferred_element_type=F32) + bgk_ref[...]
        la = (jnp.minimum(z, 0.0) - jnp.log(1.0 + jnp.exp(-jnp.abs(z)))) * (LOG2E / GLA_GATE_TAU)
        g_scr[rows, :] = _chunk_cumsum(la)

    q_scr[...] = proj(wf_ref, 0, N_QA).astype(BF16)
    decay_chunk(0)
    for c in range(2):
        a = SEG_QKG[0] + 512 * c
        qkg_scr[:, 512 * c:512 * (c + 1)] = proj(wf_ref, a, a + 512).astype(BF16)
        decay_chunk(1 + c)
    for c in range(2):
        a = SEG_VG[0] + 512 * c
        vg_scr[:, 512 * c:512 * (c + 1)] = proj(wf_ref, a, a + 512).astype(BF16)
        if c == 0:
            decay_chunk(3)

    def act_cols(c):
        r = proj(wa_ref, 512 * c, 512 * (c + 1))
        s = _sigmoid(r)
        if c < 2:
            s = r * s
        act_ref[:, 512 * c:512 * (c + 1)] = s.astype(BF16)

    lv = lv_ref[...]
    n_act = N_ACT // 512
    pieces = TM_MIX // BLOCK
    for j in range(pieces):
        att = _attn_probs(j, first, q_scr, kv_scr, bias_ref)
        per_head = _gla_scores(j, qkg_scr, vg_scr, g_scr, lv)
        act_cols(j)
        _attn_out(j, *att, oa_ref)
        _gla_state(j, per_head, s_scr, og_ref)
    for c in range(pieces, n_act):
        act_cols(c)

    kv_scr[0:BLOCK, :] = kv_scr[TM_MIX:TM_MIX + BLOCK, :]


def _mixer_call(x2, seq_len, nm, w_front, w_act, w_gk, w_gk2, b_gk, bias, lv):
    T = x2.shape[0]
    tm = TM_MIX
    full = lambda a: pl.BlockSpec(a.shape, lambda i: (0,) * a.ndim)
    rows = lambda n: pl.BlockSpec((tm, n), lambda i: (i, 0))
    consts = [nm, w_front, w_act, w_gk, w_gk2, b_gk, bias, lv]
    return pl.pallas_call(
        partial(_mixer_body, seq_len // tm),
        grid=(T // tm,),
        in_specs=[rows(D_MODEL)] + [full(a) for a in consts],
        out_specs=[rows(N_QA), rows(GLA_VAL_DIM), rows(N_ACT)],
        out_shape=[
            jax.ShapeDtypeStruct((T, N_QA), BF16),
            jax.ShapeDtypeStruct((T, GLA_VAL_DIM), BF16),
            jax.ShapeDtypeStruct((T, N_ACT), BF16),
        ],
        scratch_shapes=[
            pltpu.VMEM((tm, N_QA), BF16),
            pltpu.VMEM((BLOCK + tm, 2 * LANES), BF16),
            pltpu.VMEM((tm, 2 * GLA_KEY_DIM), BF16),
            pltpu.VMEM((tm, GLA_VAL_DIM), BF16),
            pltpu.VMEM((tm, GLA_KEY_DIM), F32),
            pltpu.VMEM((GLA_HEADS, GLA_HEAD_K, GLA_HEAD_V), F32),
        ],
        compiler_params=pltpu.CompilerParams(
            dimension_semantics=("arbitrary",), vmem_limit_bytes=VMEM_LIMIT),
        name="mixers",
    )(x2, *consts)


def _out_body(x_ref, p_ref, oa_ref, og_ref, act_ref,
              wpa_ref, wpg_ref, wo_ref, wfi_ref, wfo_ref, wpgate_ref, wple_ref,
              gn_ref, nffn_ref, nple_ref, nfin_ref, out_ref, a_scr):
    ya = jnp.dot(oa_ref[...], wpa_ref[...], preferred_element_type=F32)
    parts = []
    for h in range(GLA_HEADS):
        cols = slice(h * GLA_HEAD_V, (h + 1) * GLA_HEAD_V)
        oh = _rms(og_ref[:, cols].astype(F32), gn_ref[...]) * act_ref[:, cols].astype(F32)
        parts.append(oh.astype(BF16))
    yg = jnp.dot(jnp.concatenate(parts, axis=1), wpg_ref[...], preferred_element_type=F32)
    mixed = (act_ref[:, 1024:2048].astype(F32) * ya + act_ref[:, 2048:3072].astype(F32) * yg)
    x1 = x_ref[...] + jnp.dot(mixed.astype(BF16), wo_ref[...], preferred_element_type=F32)

    h2 = _rms(x1, nffn_ref[...]).astype(BF16)
    w = FFN_COLS
    for c in range(D_FF // w):
        w_gu = jnp.concatenate(
            [wfi_ref[:, w * c:w * (c + 1)], wfi_ref[:, D_FF + w * c:D_FF + w * (c + 1)]], axis=1)
        gu = jnp.dot(h2, w_gu, preferred_element_type=F32)
        gg = gu[:, :w]
        a_scr[:, w * c:w * (c + 1)] = (gg * _sigmoid(gg) * gu[:, w:]).astype(BF16)
    x2 = x1 + jnp.dot(a_scr[...], wfo_ref[...], preferred_element_type=F32)

    h3 = _rms(x2, nple_ref[...]).astype(BF16)
    gate = _sigmoid(jnp.dot(h3, wpgate_ref[...], preferred_element_type=F32))
    pe = jnp.dot(p_ref[...].astype(BF16), wple_ref[...], preferred_element_type=F32)
    x3 = x2 + gate * pe
    out_ref[...] = _rms(x3, nfin_ref[...])


def _out_call(x2, p2, oa, og, act, weights, gains):
    T = x2.shape[0]
    tm = TM_OUT
    full = lambda a: pl.BlockSpec(a.shape, lambda i: (0,) * a.ndim)
    rows = lambda n: pl.BlockSpec((tm, n), lambda i: (i, 0))
    return pl.pallas_call(
        _out_body,
        grid=(T // tm,),
        in_specs=[rows(D_MODEL), rows(PLE_DIM), rows(N_QA), rows(GLA_VAL_DIM), rows(N_ACT)]
                 + [full(w) for w in weights] + [full(gn) for gn in gains],
        out_specs=rows(D_MODEL),
        out_shape=jax.ShapeDtypeStruct((T, D_MODEL), F32),
        scratch_shapes=[pltpu.VMEM((tm, D_FF), BF16)],
        compiler_params=pltpu.CompilerParams(
            dimension_semantics=("arbitrary",), vmem_limit_bytes=VMEM_LIMIT),
        name="merge_ffn",
    )(x2, p2, oa, og, act, *weights, *gains)


def _t5_bucket_table():
    q_loc = np.arange(BLOCK)[:, None] + BLOCK
    k_loc = np.arange(2 * BLOCK)[None, :]
    dist = np.maximum(q_loc - k_loc, 0)
    max_exact = NUM_BUCKETS // 2
    d_f = np.maximum(dist, max_exact).astype(np.float32)
    large = max_exact + (np.log(d_f / max_exact) / math.log(MAX_DISTANCE / max_exact)
                         * (NUM_BUCKETS - max_exact)).astype(np.int32)
    large = np.minimum(large, NUM_BUCKETS - 1)
    bucket = np.where(dist < max_exact, dist, large)
    raw = q_loc - k_loc
    in_window = (raw >= 0) & (raw < WINDOW)
    return bucket.astype(np.int32), in_window


def _level_table():
    r = np.arange(GLA_CHUNK)[:, None]
    c = np.arange(GLA_CHUNK)[None, :]
    x = np.bitwise_xor(r, c)
    lv = np.where(x > 0, np.floor(np.log2(np.maximum(x, 1))).astype(np.int32), GLA_LEVELS)
    return np.where(r >= c, lv, -1).astype(np.int32)


def _front_col_scale():
    s = np.ones((N_FRONT,), np.float32)
    s[0:N_QA] = HEAD_DIM ** -0.5
    s[SEG_QKG[0]:SEG_QKG[0] + GLA_KEY_DIM] = GLA_HEAD_K ** -0.5
    return s


def _attn_bias(rel_table, sinks):
    bucket, in_window = _t5_bucket_table()
    bucket = jnp.asarray(bucket)
    rel = rel_table.astype(F32)
    bias = jnp.zeros((ATTN_HEADS, BLOCK, 2 * BLOCK), F32)
    for b in range(NUM_BUCKETS):
        bias = jnp.where(bucket[None] == b, rel[b][:, None, None], bias)
    bias = jnp.where(jnp.asarray(in_window)[None], bias, NEG)
    col0 = jnp.asarray(np.arange(2 * BLOCK) == 0)[None, None, :]
    return jnp.where(col0, sinks.astype(F32)[:, None, None], bias)


def kernel(x, p, w_in, w_gk2, b_gk, sinks, rel_table, w_proj_attn, w_proj_gla, gla_norm, w_out,
           norm_mix, norm_ffn, w_ffn_in, w_ffn_out, norm_ple, w_ple_gate, w_ple, norm_final):
    B, S, D = x.shape
    T = B * S
    assert w_in.shape[0] == 1, "single-layer trunk only"
    assert S % TM_MIX == 0 and T % TM_OUT == 0
    i = 0
    lv = jnp.asarray(_level_table())
    bias = _attn_bias(rel_table, sinks[i])

    xs = x.reshape(T, D)
    w = w_in[i]
    w_front = (w[:, :N_FRONT] * jnp.asarray(_front_col_scale())[None, :]).astype(BF16)
    w_act = w[:, GK0 + GLA_GATE_RANK:].astype(BF16)
    w_gk = jnp.pad(w[:, GK0:GK0 + GLA_GATE_RANK], ((0, 0), (0, LANES - GLA_GATE_RANK))).astype(BF16)
    w_gk2p = jnp.pad(w_gk2[i], ((0, LANES - GLA_GATE_RANK), (0, 0))).astype(BF16)

    oa, og, act = _mixer_call(
        xs, S, norm_mix[i].reshape(1, D), w_front, w_act, w_gk, w_gk2p, b_gk[i].reshape(1, -1), bias, lv)

    weights = [
        w_proj_attn[i].astype(BF16), w_proj_gla[i].astype(BF16), w_out[i].astype(BF16),
        w_ffn_in[i].astype(BF16), w_ffn_out[i].astype(BF16), w_ple_gate[i].astype(BF16), w_ple[i].astype(BF16),
    ]
    gains = [gla_norm[i].reshape(1, -1), norm_ffn[i].reshape(1, D), norm_ple[i].reshape(1, D),
             norm_final.reshape(1, D)]
    out = _out_call(xs, p[i].reshape(T, PLE_DIM), oa, og, act, weights, gains)
    return out.reshape(B, S, D)
```

```python
import math
from functools import partial

import numpy as np
import jax
import jax.numpy as jnp
from jax import lax
from jax.experimental import pallas as pl
from jax.experimental.pallas import tpu as pltpu

F32 = jnp.float32
BF16 = jnp.bfloat16

D_MODEL = 1024
ATTN_HEADS = 8
ATTN_KV_HEADS = 2
ATTN_GROUP = ATTN_HEADS // ATTN_KV_HEADS
HEAD_DIM = 64
WINDOW = 128
BLOCK = 128
NUM_BUCKETS = 32
MAX_DISTANCE = 128
GLA_HEADS = 4
GLA_KEY_DIM = D_MODEL // 2
GLA_VAL_DIM = D_MODEL
GLA_HEAD_K = GLA_KEY_DIM // GLA_HEADS
GLA_HEAD_V = GLA_VAL_DIM // GLA_HEADS
GLA_GATE_RANK = 16
GLA_GATE_TAU = 16.0
D_FF = -(-8 * D_MODEL // (3 * 256)) * 256
PLE_DIM = 256
EPS = 1e-6
NEG = -1e30
LOG2E = math.log2(math.e)

LANES = 128
SUBLANES = 8

TM_MIX = 512
GLA_CHUNK = 128
GLA_LEVELS = 7
TM_OUT = 512
FFN_COLS = 256
VMEM_LIMIT = 56 * 1024 * 1024

N_QA = ATTN_HEADS * HEAD_DIM
N_QKVA = 768
SEG_QKG = (768, 1792)
SEG_VG = (1792, 2816)
GK0 = 2816
N_FRONT = 2816
N_ACT = 3072

NT = (((1,), (1,)), ((), ()))


def _sigmoid(x):
    return 0.5 * jnp.tanh(0.5 * x) + 0.5


def _rms(x, gain):
    ms = jnp.mean(x * x, axis=-1, keepdims=True)
    return x * lax.rsqrt(ms + EPS) * gain


def _chunk_cumsum(la):
    tm, n = la.shape
    groups = tm // SUBLANES
    x = la.reshape(groups, SUBLANES, n)
    sub = lax.broadcasted_iota(jnp.int32, x.shape, 1)
    for s in (1, 2, 4):
        x = x + jnp.where(sub >= s, pltpu.roll(x, s, 1), 0.0)
    per_chunk = GLA_CHUNK // SUBLANES
    out = []
    carry = None
    for i in range(groups):
        blk = x[i]
        if i % per_chunk:
            blk = blk + carry
        out.append(blk)
        carry = jnp.broadcast_to(blk[SUBLANES - 1:SUBLANES, :], (SUBLANES, n))
    return jnp.concatenate(out, axis=0)


def _attn_probs(j, first, q_scr, kv_scr, bias_ref):
    r0 = j * BLOCK
    band_row = lax.broadcasted_iota(jnp.int32, (2 * BLOCK, LANES), 0)
    low = lax.broadcasted_iota(jnp.int32, (2 * BLOCK, LANES), 1) < HEAD_DIM
    ones = jnp.ones((2 * BLOCK, LANES), BF16)
    kband = kv_scr[r0:r0 + 2 * BLOCK, 0:LANES]
    vband = kv_scr[r0:r0 + 2 * BLOCK, LANES:2 * LANES]
    zero = jnp.zeros_like(kband)
    kband = jnp.where(band_row == 0, zero, kband)
    vband = jnp.where(band_row == 0, zero, vband)
    kroll = pltpu.roll(kband, HEAD_DIM, 1)
    vroll = pltpu.roll(vband, HEAD_DIM, 1)
    ktile = ((jnp.where(low, kband, zero), jnp.where(low, zero, kroll)),
             (jnp.where(low, kroll, zero), jnp.where(low, zero, kband)))
    if j == 0:
        col = lax.broadcasted_iota(jnp.int32, (BLOCK, 2 * BLOCK), 1)
        no_prev = jnp.logical_and(first, jnp.logical_and(col >= 1, col < BLOCK))
    probs = [[None, None], [None, None]]
    for kv in range(ATTN_KV_HEADS):
        qg = q_scr[r0:r0 + BLOCK, 2 * kv * LANES:(2 * kv + 2) * LANES]
        qg = jnp.concatenate([qg[:, :LANES], qg[:, LANES:]], axis=0)
        for odd in range(2):
            s2 = lax.dot_general(qg, ktile[kv][odd], NT, preferred_element_type=F32)
            ps = []
            for half in range(2):
                head = 2 * (2 * kv + half) + odd
                s = s2[half * BLOCK:(half + 1) * BLOCK] + bias_ref[head]
                if j == 0:
                    s = jnp.where(no_prev, NEG, s)
                ps.append(jnp.exp(s - jnp.max(s, axis=-1, keepdims=True)).astype(BF16))
            probs[kv][odd] = jnp.concatenate(ps, axis=0)
    vnat = jnp.concatenate([vband, ones], axis=1)
    vrol = jnp.concatenate([vroll, ones], axis=1)
    p_nat = jnp.concatenate([probs[0][0], probs[1][1]], axis=0)
    p_rol = jnp.concatenate([probs[0][1], probs[1][0]], axis=0)
    return p_nat, vnat, p_rol, vrol


def _attn_out(j, p_nat, vnat, p_rol, vrol, oa_ref):
    r0 = j * BLOCK
    low_o = lax.broadcasted_iota(jnp.int32, (BLOCK, LANES), 1) < HEAD_DIM
    od_nat = jnp.dot(p_nat, vnat, preferred_element_type=F32)
    od_rol = jnp.dot(p_rol, vrol, preferred_element_type=F32)
    for blk in range(ATTN_HEADS // 2):
        rows = slice(blk * BLOCK, (blk + 1) * BLOCK)
        o_nat = od_nat[rows, :LANES] / od_nat[rows, LANES:]
        o_rol = od_rol[rows, :LANES] / od_rol[rows, LANES:]
        even, odd_ = (o_nat, o_rol) if blk < ATTN_HEADS // 4 else (o_rol, o_nat)
        oa_ref[r0:r0 + BLOCK, blk * LANES:(blk + 1) * LANES] = jnp.where(low_o, even, odd_).astype(BF16)


def _anchor_small(grow, b):
    C = GLA_CHUNK

    def row(r):
        return jnp.broadcast_to(grow(r), (SUBLANES, LANES))

    sub = lax.broadcasted_iota(jnp.int32, (SUBLANES, LANES), 0)
    parts = []
    for i in range(C // SUBLANES):
        r0 = i * SUBLANES
        if b == 4:
            a = row(r0 + 3)
        elif b == 2:
            a = jnp.where(sub < 4, row(r0 + 1), row(r0 + 5))
        else:
            a = jnp.where(sub < 2, row(r0), jnp.where(sub < 4, row(r0 + 2),
                                                      jnp.where(sub < 6, row(r0 + 4), row(r0 + 6))))
        parts.append(a)
    return jnp.concatenate(parts, axis=0)


def _level_operand(q, k, g, grow, b, row):
    C = GLA_CHUNK
    if b >= SUBLANES:
        expo, src = [], []
        for p in range(C // (2 * b)):
            lo, mid, hi = 2 * b * p, 2 * b * p + b, 2 * b * (p + 1)
            a = jnp.broadcast_to(grow(mid - 1), (b, LANES))
            expo += [a - g[lo:mid], g[mid:hi] - a]
            src += [k[lo:mid], q[mid:hi]]
        expo = jnp.concatenate(expo, axis=0)
        src = jnp.concatenate(src, axis=0)
    else:
        upper = (row & b) != 0
        d = g - _anchor_small(grow, b)
        expo = jnp.where(upper, d, -d)
        src = jnp.where(upper, q, k)
    return (src * jnp.exp2(expo)).astype(BF16)


def _gla_scores(c, qkg_scr, vg_scr, g_scr, lv):
    C = GLA_CHUNK
    r0 = c * C
    row = lax.broadcasted_iota(jnp.int32, (C, LANES), 0)
    per_head = []
    for h in range(GLA_HEADS):
        cols = slice(h * GLA_HEAD_K, (h + 1) * GLA_HEAD_K)
        grow = lambda r, cols=cols: g_scr[r0 + r:r0 + r + 1, cols]
        q_b = qkg_scr[r0:r0 + C, cols]
        k_b = qkg_scr[r0:r0 + C, GLA_KEY_DIM + h * GLA_HEAD_K:GLA_KEY_DIM + (h + 1) * GLA_HEAD_K]
        q = q_b.astype(F32)
        k = k_b.astype(F32)
        v = vg_scr[r0:r0 + C, h * GLA_HEAD_V:(h + 1) * GLA_HEAD_V]
        g = g_scr[r0:r0 + C, cols]
        glast = jnp.broadcast_to(grow(C - 1), (C, GLA_HEAD_K))

        scores = jnp.where(lv == GLA_LEVELS, lax.dot_general(q_b, k_b, NT, preferred_element_type=F32), 0.0)
        for l in range(GLA_LEVELS):
            r = _level_operand(q, k, g, grow, 1 << l, row)
            scores = jnp.where(lv == l, lax.dot_general(r, r, NT, preferred_element_type=F32), scores)
        per_head.append((q, k, v, g, glast, scores))
    return per_head


def _gla_state(c, per_head, s_scr, og_ref):
    C = GLA_CHUNK
    r0 = c * C
    for h, (q, k, v, g, glast, scores) in enumerate(per_head):
        state = s_scr[h]
        qt = (q * jnp.exp2(g)).astype(BF16)
        lhs = jnp.concatenate([qt, scores.astype(BF16)], axis=1)
        rhs = jnp.concatenate([state.astype(BF16), v], axis=0)
        o = jnp.dot(lhs, rhs, preferred_element_type=F32)
        og_ref[r0:r0 + C, h * GLA_HEAD_V:(h + 1) * GLA_HEAD_V] = o.astype(BF16)

        kd_t = (k * jnp.exp2(glast - g)).T.astype(BF16)
        gam = jnp.exp2(glast.T)
        upd = jnp.dot(kd_t, v, preferred_element_type=F32)
        s_scr[h] = state * jnp.concatenate([gam, gam], axis=1) + upd


def _mixer_body(tiles_per_seq, x_ref, nm_ref, wf_ref, wa_ref, wgk_ref, wgk2_ref, bgk_ref, bias_ref, lv_ref,
                oa_ref, og_ref, act_ref, q_scr, kv_scr, qkg_scr, vg_scr, g_scr, s_scr):
    i = pl.program_id(0)
    first = (i % tiles_per_seq) == 0

    @pl.when(i == 0)
    def _():
        kv_scr[0:BLOCK, :] = jnp.zeros((BLOCK, 2 * LANES), BF16)

    @pl.when(first)
    def _():
        s_scr[...] = jnp.zeros_like(s_scr)

    h = _rms(x_ref[...], nm_ref[...]).astype(BF16)

    def proj(w_ref, c0, c1):
        return jnp.dot(h, w_ref[:, c0:c1], preferred_element_type=F32)

    w_kvgk = jnp.concatenate(
        [wf_ref[:, N_QA:N_QKVA], wgk_ref[...], jnp.zeros((D_MODEL, LANES), BF16)], axis=1)
    kvgk = jnp.dot(h, w_kvgk, preferred_element_type=F32)
    kv_scr[BLOCK:BLOCK + TM_MIX, :] = kvgk[:, :2 * LANES].astype(BF16)
    gk = kvgk[:, 2 * LANES:3 * LANES].astype(BF16)

    def decay_chunk(j):
        rows = slice(j * GLA_CHUNK, (j + 1) * GLA_CHUNK)
        z = jnp.dot(gk[rows], wgk2_ref[...], preferred_element_type=F32) + bgk_ref[...]
        la = (jnp.minimum(z, 0.0) - jnp.log(1.0 + jnp.exp(-jnp.abs(z)))) * (LOG2E / GLA_GATE_TAU)
        g_scr[rows, :] = _chunk_cumsum(la)

    q_scr[...] = proj(wf_ref, 0, N_QA).astype(BF16)
    decay_chunk(0)
    for c in range(2):
        a = SEG_QKG[0] + 512 * c
        qkg_scr[:, 512 * c:512 * (c + 1)] = proj(wf_ref, a, a + 512).astype(BF16)
        decay_chunk(1 + c)
    for c in range(2):
        a = SEG_VG[0] + 512 * c
        vg_scr[:, 512 * c:512 * (c + 1)] = proj(wf_ref, a, a + 512).astype(BF16)
        if c == 0:
            decay_chunk(3)

    def act_cols(c):
        r = proj(wa_ref, 512 * c, 512 * (c + 1))
        s = _sigmoid(r)
        if c < 2:
            s = r * s
        act_ref[:, 512 * c:512 * (c + 1)] = s.astype(BF16)

    lv = lv_ref[...]
    n_act = N_ACT // 512
    pieces = TM_MIX // BLOCK
    for j in range(pieces):
        att = _attn_probs(j, first, q_scr, kv_scr, bias_ref)
        per_head = _gla_scores(j, qkg_scr, vg_scr, g_scr, lv)
        act_cols(j)
        _attn_out(j, *att, oa_ref)
        _gla_state(j, per_head, s_scr, og_ref)
    for c in range(pieces, n_act):
        act_cols(c)

    kv_scr[0:BLOCK, :] = kv_scr[TM_MIX:TM_MIX + BLOCK, :]


def _mixer_call(x2, seq_len, nm, w_front, w_act, w_gk, w_gk2, b_gk, bias, lv):
    T = x2.shape[0]
    tm = TM_MIX
    full = lambda a: pl.BlockSpec(a.shape, lambda i: (0,) * a.ndim)
    rows = lambda n: pl.BlockSpec((tm, n), lambda i: (i, 0))
    consts = [nm, w_front, w_act, w_gk, w_gk2, b_gk, bias, lv]
    return pl.pallas_call(
        partial(_mixer_body, seq_len // tm),
        grid=(T // tm,),
        in_specs=[rows(D_MODEL)] + [full(a) for a in consts],
        out_specs=[rows(N_QA), rows(GLA_VAL_DIM), rows(N_ACT)],
        out_shape=[
            jax.ShapeDtypeStruct((T, N_QA), BF16),
            jax.ShapeDtypeStruct((T, GLA_VAL_DIM), BF16),
            jax.ShapeDtypeStruct((T, N_ACT), BF16),
        ],
        scratch_shapes=[
            pltpu.VMEM((tm, N_QA), BF16),
            pltpu.VMEM((BLOCK + tm, 2 * LANES), BF16),
            pltpu.VMEM((tm, 2 * GLA_KEY_DIM), BF16),
            pltpu.VMEM((tm, GLA_VAL_DIM), BF16),
            pltpu.VMEM((tm, GLA_KEY_DIM), F32),
            pltpu.VMEM((GLA_HEADS, GLA_HEAD_K, GLA_HEAD_V), F32),
        ],
        compiler_params=pltpu.CompilerParams(
            dimension_semantics=("arbitrary",), vmem_limit_bytes=VMEM_LIMIT),
        name="mixers",
    )(x2, *consts)


def _out_body(x_ref, p_ref, oa_ref, og_ref, act_ref,
              wpa_ref, wpg_ref, wo_ref, wfi_ref, wfo_ref, wpgate_ref, wple_ref,
              gn_ref, nffn_ref, nple_ref, nfin_ref, out_ref, a_scr):
    ya = jnp.dot(oa_ref[...], wpa_ref[...], preferred_element_type=F32)
    parts = []
    for h in range(GLA_HEADS):
        cols = slice(h * GLA_HEAD_V, (h + 1) * GLA_HEAD_V)
        oh = _rms(og_ref[:, cols].astype(F32), gn_ref[...]) * act_ref[:, cols].astype(F32)
        parts.append(oh.astype(BF16))
    yg = jnp.dot(jnp.concatenate(parts, axis=1), wpg_ref[...], preferred_element_type=F32)
    mixed = (act_ref[:, 1024:2048].astype(F32) * ya + act_ref[:, 2048:3072].astype(F32) * yg)
    x1 = x_ref[...] + jnp.dot(mixed.astype(BF16), wo_ref[...], preferred_element_type=F32)

    h2 = _rms(x1, nffn_ref[...]).astype(BF16)
    w = FFN_COLS
    for c in range(D_FF // w):
        w_gu = jnp.concatenate(
            [wfi_ref[:, w * c:w * (c + 1)], wfi_ref[:, D_FF + w * c:D_FF + w * (c + 1)]], axis=1)
        gu = jnp.dot(h2, w_gu, preferred_element_type=F32)
        gg = gu[:, :w]
        a_scr[:, w * c:w * (c + 1)] = (gg * _sigmoid(gg) * gu[:, w:]).astype(BF16)
    x2 = x1 + jnp.dot(a_scr[...], wfo_ref[...], preferred_element_type=F32)

    h3 = _rms(x2, nple_ref[...]).astype(BF16)
    gate = _sigmoid(jnp.dot(h3, wpgate_ref[...], preferred_element_type=F32))
    pe = jnp.dot(p_ref[...].astype(BF16), wple_ref[...], preferred_element_type=F32)
    x3 = x2 + gate * pe
    out_ref[...] = _rms(x3, nfin_ref[...])


def _out_call(x2, p2, oa, og, act, weights, gains):
    T = x2.shape[0]
    tm = TM_OUT
    full = lambda a: pl.BlockSpec(a.shape, lambda i: (0,) * a.ndim)
    rows = lambda n: pl.BlockSpec((tm, n), lambda i: (i, 0))
    return pl.pallas_call(
        _out_body,
        grid=(T // tm,),
        in_specs=[rows(D_MODEL), rows(PLE_DIM), rows(N_QA), rows(GLA_VAL_DIM), rows(N_ACT)]
                 + [full(w) for w in weights] + [full(gn) for gn in gains],
        out_specs=rows(D_MODEL),
        out_shape=jax.ShapeDtypeStruct((T, D_MODEL), F32),
        scratch_shapes=[pltpu.VMEM((tm, D_FF), BF16)],
        compiler_params=pltpu.CompilerParams(
            dimension_semantics=("arbitrary",), vmem_limit_bytes=VMEM_LIMIT),
        name="merge_ffn",
    )(x2, p2, oa, og, act, *weights, *gains)


def _t5_bucket_table():
    q_loc = np.arange(BLOCK)[:, None] + BLOCK
    k_loc = np.arange(2 * BLOCK)[None, :]
    dist = np.maximum(q_loc - k_loc, 0)
    max_exact = NUM_BUCKETS // 2
    d_f = np.maximum(dist, max_exact).astype(np.float32)
    large = max_exact + (np.log(d_f / max_exact) / math.log(MAX_DISTANCE / max_exact)
                         * (NUM_BUCKETS - max_exact)).astype(np.int32)
    large = np.minimum(large, NUM_BUCKETS - 1)
    bucket = np.where(dist < max_exact, dist, large)
    raw = q_loc - k_loc
    in_window = (raw >= 0) & (raw < WINDOW)
    return bucket.astype(np.int32), in_window


def _level_table():
    r = np.arange(GLA_CHUNK)[:, None]
    c = np.arange(GLA_CHUNK)[None, :]
    x = np.bitwise_xor(r, c)
    lv = np.where(x > 0, np.floor(np.log2(np.maximum(x, 1))).astype(np.int32), GLA_LEVELS)
    return np.where(r >= c, lv, -1).astype(np.int32)


def _front_col_scale():
    s = np.ones((N_FRONT,), np.float32)
    s[0:N_QA] = HEAD_DIM ** -0.5
    s[SEG_QKG[0]:SEG_QKG[0] + GLA_KEY_DIM] = GLA_HEAD_K ** -0.5
    return s


def _attn_bias(rel_table, sinks):
    bucket, in_window = _t5_bucket_table()
    bucket = jnp.asarray(bucket)
    rel = rel_table.astype(F32)
    bias = jnp.zeros((ATTN_HEADS, BLOCK, 2 * BLOCK), F32)
    for b in range(NUM_BUCKETS):
        bias = jnp.where(bucket[None] == b, rel[b][:, None, None], bias)
    bias = jnp.where(jnp.asarray(in_window)[None], bias, NEG)
    col0 = jnp.asarray(np.arange(2 * BLOCK) == 0)[None, None, :]
    return jnp.where(col0, sinks.astype(F32)[:, None, None], bias)


def kernel(x, p, w_in, w_gk2, b_gk, sinks, rel_table, w_proj_attn, w_proj_gla, gla_norm, w_out,
           norm_mix, norm_ffn, w_ffn_in, w_ffn_out, norm_ple, w_ple_gate, w_ple, norm_final):
    B, S, D = x.shape
    T = B * S
    assert w_in.shape[0] == 1, "single-layer trunk only"
    assert S % TM_MIX == 0 and T % TM_OUT == 0
    i = 0
    lv = jnp.asarray(_level_table())
    bias = _attn_bias(rel_table, sinks[i])

    xs = x.reshape(T, D)
    w = w_in[i]
    w_front = (w[:, :N_FRONT] * jnp.asarray(_front_col_scale())[None, :]).astype(BF16)
    w_act = w[:, GK0 + GLA_GATE_RANK:].astype(BF16)
    w_gk = jnp.pad(w[:, GK0:GK0 + GLA_GATE_RANK], ((0, 0), (0, LANES - GLA_GATE_RANK))).astype(BF16)
    w_gk2p = jnp.pad(w_gk2[i], ((0, LANES - GLA_GATE_RANK), (0, 0))).astype(BF16)

    oa, og, act = _mixer_call(
        xs, S, norm_mix[i].reshape(1, D), w_front, w_act, w_gk, w_gk2p, b_gk[i].reshape(1, -1), bias, lv)

    weights = [
        w_proj_attn[i].astype(BF16), w_proj_gla[i].astype(BF16), w_out[i].astype(BF16),
        w_ffn_in[i].astype(BF16), w_ffn_out[i].astype(BF16), w_ple_gate[i].astype(BF16), w_ple[i].astype(BF16),
    ]
    gains = [gla_norm[i].reshape(1, -1), norm_ffn[i].reshape(1, D), norm_ple[i].reshape(1, D),
             norm_final.reshape(1, D)]
    out = _out_call(xs, p[i].reshape(T, PLE_DIM), oa, og, act, weights, gains)
    return out.reshape(B, S, D)
```

```python
import math
from functools import partial

import numpy as np
import jax
import jax.numpy as jnp
from jax import lax
from jax.experimental import pallas as pl
from jax.experimental.pallas import tpu as pltpu

F32 = jnp.float32
BF16 = jnp.bfloat16

D_MODEL = 1024
ATTN_HEADS = 8
ATTN_KV_HEADS = 2
ATTN_GROUP = ATTN_HEADS // ATTN_KV_HEADS
HEAD_DIM = 64
WINDOW = 128
BLOCK = 128
NUM_BUCKETS = 32
MAX_DISTANCE = 128
GLA_HEADS = 4
GLA_KEY_DIM = D_MODEL // 2
GLA_VAL_DIM = D_MODEL
GLA_HEAD_K = GLA_KEY_DIM // GLA_HEADS
GLA_HEAD_V = GLA_VAL_DIM // GLA_HEADS
GLA_GATE_RANK = 16
GLA_GATE_TAU = 16.0
D_FF = -(-8 * D_MODEL // (3 * 256)) * 256
PLE_DIM = 256
EPS = 1e-6
NEG = -1e30
LOG2E = math.log2(math.e)

LANES = 128
SUBLANES = 8

TM_MIX = 512
GLA_CHUNK = 128
GLA_LEVELS = 7
TM_OUT = 512
FFN_COLS = 256
VMEM_LIMIT = 56 * 1024 * 1024

N_QA = ATTN_HEADS * HEAD_DIM
N_QKVA = 768
SEG_QKG = (768, 1792)
SEG_VG = (1792, 2816)
GK0 = 2816
N_FRONT = 2816
N_ACT = 3072

NT = (((1,), (1,)), ((), ()))


def _sigmoid(x):
    return 0.5 * jnp.tanh(0.5 * x) + 0.5


def _rms(x, gain):
    ms = jnp.mean(x * x, axis=-1, keepdims=True)
    return x * lax.rsqrt(ms + EPS) * gain


def _chunk_cumsum(la):
    tm, n = la.shape
    groups = tm // SUBLANES
    x = la.reshape(groups, SUBLANES, n)
    sub = lax.broadcasted_iota(jnp.int32, x.shape, 1)
    for s in (1, 2, 4):
        x = x + jnp.where(sub >= s, pltpu.roll(x, s, 1), 0.0)
    per_chunk = GLA_CHUNK // SUBLANES
    out = []
    carry = None
    for i in range(groups):
        blk = x[i]
        if i % per_chunk:
            blk = blk + carry
        out.append(blk)
        carry = jnp.broadcast_to(blk[SUBLANES - 1:SUBLANES, :], (SUBLANES, n))
    return jnp.concatenate(out, axis=0)


def _attn_probs(j, first, q_scr, kv_scr, bias_ref):
    r0 = j * BLOCK
    band_row = lax.broadcasted_iota(jnp.int32, (2 * BLOCK, LANES), 0)
    low = lax.broadcasted_iota(jnp.int32, (2 * BLOCK, LANES), 1) < HEAD_DIM
    ones = jnp.ones((2 * BLOCK, LANES), BF16)
    kband = kv_scr[r0:r0 + 2 * BLOCK, 0:LANES]
    vband = kv_scr[r0:r0 + 2 * BLOCK, LANES:2 * LANES]
    zero = jnp.zeros_like(kband)
    kband = jnp.where(band_row == 0, zero, kband)
    vband = jnp.where(band_row == 0, zero, vband)
    kroll = pltpu.roll(kband, HEAD_DIM, 1)
    vroll = pltpu.roll(vband, HEAD_DIM, 1)
    ktile = ((jnp.where(low, kband, zero), jnp.where(low, zero, kroll)),
             (jnp.where(low, kroll, zero), jnp.where(low, zero, kband)))
    if j == 0:
        col = lax.broadcasted_iota(jnp.int32, (BLOCK, 2 * BLOCK), 1)
        no_prev = jnp.logical_and(first, jnp.logical_and(col >= 1, col < BLOCK))
    probs = [[None, None], [None, None]]
    for kv in range(ATTN_KV_HEADS):
        qg = q_scr[r0:r0 + BLOCK, 2 * kv * LANES:(2 * kv + 2) * LANES]
        qg = jnp.concatenate([qg[:, :LANES], qg[:, LANES:]], axis=0)
        for odd in range(2):
            s2 = lax.dot_general(qg, ktile[kv][odd], NT, preferred_element_type=F32)
            ps = []
            for half in range(2):
                head = 2 * (2 * kv + half) + odd
                s = s2[half * BLOCK:(half + 1) * BLOCK] + bias_ref[head]
                if j == 0:
                    s = jnp.where(no_prev, NEG, s)
                ps.append(jnp.exp(s - jnp.max(s, axis=-1, keepdims=True)).astype(BF16))
            probs[kv][odd] = jnp.concatenate(ps, axis=0)
    vnat = jnp.concatenate([vband, ones], axis=1)
    vrol = jnp.concatenate([vroll, ones], axis=1)
    p_nat = jnp.concatenate([probs[0][0], probs[1][1]], axis=0)
    p_rol = jnp.concatenate([probs[0][1], probs[1][0]], axis=0)
    return p_nat, vnat, p_rol, vrol


def _attn_out(j, p_nat, vnat, p_rol, vrol, oa_ref):
    r0 = j * BLOCK
    low_o = lax.broadcasted_iota(jnp.int32, (BLOCK, LANES), 1) < HEAD_DIM
    od_nat = jnp.dot(p_nat, vnat, preferred_element_type=F32)
    od_rol = jnp.dot(p_rol, vrol, preferred_element_type=F32)
    for blk in range(ATTN_HEADS // 2):
        rows = slice(blk * BLOCK, (blk + 1) * BLOCK)
        o_nat = od_nat[rows, :LANES] / od_nat[rows, LANES:]
        o_rol = od_rol[rows, :LANES] / od_rol[rows, LANES:]
        even, odd_ = (o_nat, o_rol) if blk < ATTN_HEADS // 4 else (o_rol, o_nat)
        oa_ref[r0:r0 + BLOCK, blk * LANES:(blk + 1) * LANES] = jnp.where(low_o, even, odd_).astype(BF16)


def _anchor_small(grow, b):
    C = GLA_CHUNK

    def row(r):
        return jnp.broadcast_to(grow(r), (SUBLANES, LANES))

    sub = lax.broadcasted_iota(jnp.int32, (SUBLANES, LANES), 0)
    parts = []
    for i in range(C // SUBLANES):
        r0 = i * SUBLANES
        if b == 4:
            a = row(r0 + 3)
        elif b == 2:
            a = jnp.where(sub < 4, row(r0 + 1), row(r0 + 5))
        else:
            a = jnp.where(sub < 2, row(r0), jnp.where(sub < 4, row(r0 + 2),
                                                      jnp.where(sub < 6, row(r0 + 4), row(r0 + 6))))
        parts.append(a)
    return jnp.concatenate(parts, axis=0)


def _level_operand(q, k, g, grow, b, row):
    C = GLA_CHUNK
    if b >= SUBLANES:
        expo, src = [], []
        for p in range(C // (2 * b)):
            lo, mid, hi = 2 * b * p, 2 * b * p + b, 2 * b * (p + 1)
            a = jnp.broadcast_to(grow(mid - 1), (b, LANES))
            expo += [a - g[lo:mid], g[mid:hi] - a]
            src += [k[lo:mid], q[mid:hi]]
        expo = jnp.concatenate(expo, axis=0)
        src = jnp.concatenate(src, axis=0)
    else:
        upper = (row & b) != 0
        d = g - _anchor_small(grow, b)
        expo = jnp.where(upper, d, -d)
        src = jnp.where(upper, q, k)
    return (src * jnp.exp2(expo)).astype(BF16)


def _gla_scores(c, qkg_scr, vg_scr, g_scr, lv):
    C = GLA_CHUNK
    r0 = c * C
    row = lax.broadcasted_iota(jnp.int32, (C, LANES), 0)
    per_head = []
    for h in range(GLA_HEADS):
        cols = slice(h * GLA_HEAD_K, (h + 1) * GLA_HEAD_K)
        grow = lambda r, cols=cols: g_scr[r0 + r:r0 + r + 1, cols]
        q_b = qkg_scr[r0:r0 + C, cols]
        k_b = qkg_scr[r0:r0 + C, GLA_KEY_DIM + h * GLA_HEAD_K:GLA_KEY_DIM + (h + 1) * GLA_HEAD_K]
        q = q_b.astype(F32)
        k = k_b.astype(F32)
        v = vg_scr[r0:r0 + C, h * GLA_HEAD_V:(h + 1) * GLA_HEAD_V]
        g = g_scr[r0:r0 + C, cols]
        glast = jnp.broadcast_to(grow(C - 1), (C, GLA_HEAD_K))

        scores = jnp.where(lv == GLA_LEVELS, lax.dot_general(q_b, k_b, NT, preferred_element_type=F32), 0.0)
        for l in range(GLA_LEVELS):
            r = _level_operand(q, k, g, grow, 1 << l, row)
            scores = jnp.where(lv == l, lax.dot_general(r, r, NT, preferred_element_type=F32), scores)
        per_head.append((q, k, v, g, glast, scores))
    return per_head


def _gla_state(c, per_head, s_scr, og_ref):
    C = GLA_CHUNK
    r0 = c * C
    for h, (q, k, v, g, glast, scores) in enumerate(per_head):
        state = s_scr[h]
        qt = (q * jnp.exp2(g)).astype(BF16)
        lhs = jnp.concatenate([qt, scores.astype(BF16)], axis=1)
        rhs = jnp.concatenate([state.astype(BF16), v], axis=0)
        o = jnp.dot(lhs, rhs, preferred_element_type=F32)
        og_ref[r0:r0 + C, h * GLA_HEAD_V:(h + 1) * GLA_HEAD_V] = o.astype(BF16)

        kd_t = (k * jnp.exp2(glast - g)).T.astype(BF16)
        gam = jnp.exp2(glast.T)
        upd = jnp.dot(kd_t, v, preferred_element_type=F32)
        s_scr[h] = state * jnp.concatenate([gam, gam], axis=1) + upd


def _mixer_body(tiles_per_seq, x_ref, nm_ref, wf_ref, wa_ref, wgk2_ref, bgk_ref, bias_ref, lv_ref,
                oa_ref, og_ref, gate_ref, q_scr, kv_scr, qkg_scr, vg_scr, g_scr, s_scr):
    i = pl.program_id(0)
    first = (i % tiles_per_seq) == 0

    @pl.when(i == 0)
    def _():
        kv_scr[0:BLOCK, :] = jnp.zeros((BLOCK, 2 * LANES), BF16)

    @pl.when(first)
    def _():
        s_scr[...] = jnp.zeros_like(s_scr)

    h = _rms(x_ref[...], nm_ref[...]).astype(BF16)

    def proj(w_ref, c0, c1):
        return jnp.dot(h, w_ref[:, c0:c1], preferred_element_type=F32)

    w_kvgk = jnp.concatenate(
        [wf_ref[:, N_QA:N_QKVA], wf_ref[:, GK0:GK0 + LANES], jnp.zeros((D_MODEL, LANES), BF16)], axis=1)
    kvgk = jnp.dot(h, w_kvgk, preferred_element_type=F32)
    kv_scr[BLOCK:BLOCK + TM_MIX, :] = kvgk[:, :2 * LANES].astype(BF16)
    gk = kvgk[:, 2 * LANES:3 * LANES].astype(BF16)

    def decay_chunk(j):
        rows = slice(j * GLA_CHUNK, (j + 1) * GLA_CHUNK)
        z = jnp.dot(gk[rows], wgk2_ref[...], preferred_element_type=F32) + bgk_ref[...]
        la = (jnp.minimum(z, 0.0) - jnp.log(1.0 + jnp.exp(-jnp.abs(z)))) * (LOG2E / GLA_GATE_TAU)
        g_scr[rows, :] = _chunk_cumsum(la)

    q_scr[...] = (proj(wf_ref, 0, N_QA) * (HEAD_DIM ** -0.5)).astype(BF16)

    def gla_proj(j):
        if j < 2:
            a = SEG_QKG[0] + 512 * j
            r = proj(wf_ref, a, a + 512)
            if j == 0:
                r = r * (GLA_HEAD_K ** -0.5)
            qkg_scr[:, 512 * j:512 * (j + 1)] = r.astype(BF16)
        else:
            a = SEG_VG[0] + 512 * (j - 2)
            vg_scr[:, 512 * (j - 2):512 * (j - 1)] = proj(wf_ref, a, a + 512).astype(BF16)

    for j in range(TM_MIX // BLOCK):
        att = _attn_probs(j, first, q_scr, kv_scr, bias_ref)
        gla_proj(j)
        decay_chunk(j)
        _attn_out(j, *att, oa_ref)

    def act_cols(c):
        r = proj(wa_ref, 512 * c, 512 * (c + 1))
        s = _sigmoid(r)
        if c < 2:
            s = r * s
        gate_ref[:, 512 * c:512 * (c + 1)] = s.astype(BF16)

    lv = lv_ref[...]
    n_act = N_ACT // 512
    chunks = TM_MIX // GLA_CHUNK
    act_of_chunk = [list(range(n_act))[j::chunks] for j in range(chunks)]
    for j in range(chunks):
        per_head = _gla_scores(j, qkg_scr, vg_scr, g_scr, lv)
        for c in act_of_chunk[j]:
            act_cols(c)
        _gla_state(j, per_head, s_scr, og_ref)

    kv_scr[0:BLOCK, :] = kv_scr[TM_MIX:TM_MIX + BLOCK, :]


def _mixer_call(x2, seq_len, nm, w_all, w_act, w_gk2, b_gk, bias, lv):
    T = x2.shape[0]
    tm = TM_MIX
    full = lambda a: pl.BlockSpec(a.shape, lambda i: (0,) * a.ndim)
    rows = lambda n: pl.BlockSpec((tm, n), lambda i: (i, 0))
    front = pl.BlockSpec((D_MODEL, GK0 + LANES), lambda i: (0, 0))
    consts = [nm, w_all, w_act, w_gk2, b_gk, bias, lv]
    return pl.pallas_call(
        partial(_mixer_body, seq_len // tm),
        grid=(T // tm,),
        in_specs=[rows(D_MODEL), full(nm), front] + [full(a) for a in consts[2:]],
        out_specs=[rows(N_QA), rows(GLA_VAL_DIM), rows(N_ACT)],
        out_shape=[
            jax.ShapeDtypeStruct((T, N_QA), BF16),
            jax.ShapeDtypeStruct((T, GLA_VAL_DIM), BF16),
            jax.ShapeDtypeStruct((T, N_ACT), BF16),
        ],
        scratch_shapes=[
            pltpu.VMEM((tm, N_QA), BF16),
            pltpu.VMEM((BLOCK + tm, 2 * LANES), BF16),
            pltpu.VMEM((tm, 2 * GLA_KEY_DIM), BF16),
            pltpu.VMEM((tm, GLA_VAL_DIM), BF16),
            pltpu.VMEM((tm, GLA_KEY_DIM), F32),
            pltpu.VMEM((GLA_HEADS, GLA_HEAD_K, GLA_HEAD_V), F32),
        ],
        compiler_params=pltpu.CompilerParams(
            dimension_semantics=("arbitrary",), vmem_limit_bytes=VMEM_LIMIT),
        name="mixers",
    )(x2, *consts)


def _out_body(x_ref, p_ref, oa_ref, og_ref, gate_ref,
              wpa_ref, wpg_ref, wo_ref, wfi_ref, wfo_ref, wpgate_ref, wple_ref,
              gn_ref, nffn_ref, nple_ref, nfin_ref, out_ref, a_scr):
    ya = jnp.dot(oa_ref[...], wpa_ref[...], preferred_element_type=F32)
    parts = []
    for h in range(GLA_HEADS):
        cols = slice(h * GLA_HEAD_V, (h + 1) * GLA_HEAD_V)
        oh = _rms(og_ref[:, cols].astype(F32), gn_ref[...]) * gate_ref[:, cols].astype(F32)
        parts.append(oh.astype(BF16))
    yg = jnp.dot(jnp.concatenate(parts, axis=1), wpg_ref[...], preferred_element_type=F32)
    mixed = (gate_ref[:, GLA_VAL_DIM:GLA_VAL_DIM + D_MODEL].astype(F32) * ya
             + gate_ref[:, GLA_VAL_DIM + D_MODEL:N_ACT].astype(F32) * yg)
    x1 = x_ref[...] + jnp.dot(mixed.astype(BF16), wo_ref[...], preferred_element_type=F32)

    h2 = _rms(x1, nffn_ref[...]).astype(BF16)
    w = FFN_COLS
    for c in range(D_FF // w):
        w_gu = jnp.concatenate(
            [wfi_ref[:, w * c:w * (c + 1)], wfi_ref[:, D_FF + w * c:D_FF + w * (c + 1)]], axis=1)
        gu = jnp.dot(h2, w_gu, preferred_element_type=F32)
        gg = gu[:, :w]
        a_scr[:, w * c:w * (c + 1)] = (gg * _sigmoid(gg) * gu[:, w:]).astype(BF16)
    x2 = x1 + jnp.dot(a_scr[...], wfo_ref[...], preferred_element_type=F32)

    h3 = _rms(x2, nple_ref[...]).astype(BF16)
    gate = _sigmoid(jnp.dot(h3, wpgate_ref[...], preferred_element_type=F32))
    pe = jnp.dot(p_ref[...].astype(BF16), wple_ref[...], preferred_element_type=F32)
    x3 = x2 + gate * pe
    out_ref[...] = _rms(x3, nfin_ref[...])


def _out_call(x2, p2, oa, og, act, weights, gains):
    T = x2.shape[0]
    tm = TM_OUT
    full = lambda a: pl.BlockSpec(a.shape, lambda i: (0,) * a.ndim)
    rows = lambda n: pl.BlockSpec((tm, n), lambda i: (i, 0))
    return pl.pallas_call(
        _out_body,
        grid=(T // tm,),
        in_specs=[rows(D_MODEL), rows(PLE_DIM), rows(N_QA), rows(GLA_VAL_DIM), rows(N_ACT)]
                 + [full(w) for w in weights] + [full(gn) for gn in gains],
        out_specs=rows(D_MODEL),
        out_shape=jax.ShapeDtypeStruct((T, D_MODEL), F32),
        scratch_shapes=[pltpu.VMEM((tm, D_FF), BF16)],
        compiler_params=pltpu.CompilerParams(
            dimension_semantics=("arbitrary",), vmem_limit_bytes=VMEM_LIMIT),
        name="merge_ffn",
    )(x2, p2, oa, og, act, *weights, *gains)


def _t5_bucket_table():
    q_loc = np.arange(BLOCK)[:, None] + BLOCK
    k_loc = np.arange(2 * BLOCK)[None, :]
    dist = np.maximum(q_loc - k_loc, 0)
    max_exact = NUM_BUCKETS // 2
    d_f = np.maximum(dist, max_exact).astype(np.float32)
    large = max_exact + (np.log(d_f / max_exact) / math.log(MAX_DISTANCE / max_exact)
                         * (NUM_BUCKETS - max_exact)).astype(np.int32)
    large = np.minimum(large, NUM_BUCKETS - 1)
    bucket = np.where(dist < max_exact, dist, large)
    raw = q_loc - k_loc
    in_window = (raw >= 0) & (raw < WINDOW)
    return bucket.astype(np.int32), in_window


def _level_table():
    r = np.arange(GLA_CHUNK)[:, None]
    c = np.arange(GLA_CHUNK)[None, :]
    x = np.bitwise_xor(r, c)
    lv = np.where(x > 0, np.floor(np.log2(np.maximum(x, 1))).astype(np.int32), GLA_LEVELS)
    return np.where(r >= c, lv, -1).astype(np.int32)


def _attn_bias(rel_table, sinks):
    bucket, in_window = _t5_bucket_table()
    bucket = jnp.asarray(bucket)
    rel = rel_table.astype(F32)
    bias = jnp.zeros((ATTN_HEADS, BLOCK, 2 * BLOCK), F32)
    for b in range(NUM_BUCKETS):
        bias = jnp.where(bucket[None] == b, rel[b][:, None, None], bias)
    bias = jnp.where(jnp.asarray(in_window)[None], bias, NEG)
    col0 = jnp.asarray(np.arange(2 * BLOCK) == 0)[None, None, :]
    return jnp.where(col0, sinks.astype(F32)[:, None, None], bias)


def kernel(x, p, w_in, w_gk2, b_gk, sinks, rel_table, w_proj_attn, w_proj_gla, gla_norm, w_out,
           norm_mix, norm_ffn, w_ffn_in, w_ffn_out, norm_ple, w_ple_gate, w_ple, norm_final):
    B, S, D = x.shape
    T = B * S
    assert w_in.shape[0] == 1, "single-layer trunk only"
    assert S % TM_MIX == 0 and T % TM_OUT == 0
    i = 0
    lv = jnp.asarray(_level_table())
    bias = _attn_bias(rel_table, sinks[i])

    xs = x.reshape(T, D)
    w = w_in[i]
    w_all = w.astype(BF16)
    w_act = w_all[:, GK0 + GLA_GATE_RANK:]
    w_gk2p = jnp.pad(w_gk2[i], ((0, LANES - GLA_GATE_RANK), (0, 0))).astype(BF16)

    oa, og, act = _mixer_call(
        xs, S, norm_mix[i].reshape(1, D), w_all, w_act, w_gk2p, b_gk[i].reshape(1, -1), bias, lv)

    weights = [
        w_proj_attn[i].astype(BF16), w_proj_gla[i].astype(BF16), w_out[i].astype(BF16),
        w_ffn_in[i].astype(BF16), w_ffn_out[i].astype(BF16), w_ple_gate[i].astype(BF16), w_ple[i].astype(BF16),
    ]
    gains = [gla_norm[i].reshape(1, -1), norm_ffn[i].reshape(1, D), norm_ple[i].reshape(1, D),
             norm_final.reshape(1, D)]
    out = _out_call(xs, p[i].reshape(T, PLE_DIM), oa, og, act, weights, gains)
    return out.reshape(B, S, D)
```

```python
import math
from functools import partial

import numpy as np
import jax
import jax.numpy as jnp
from jax import lax
from jax.experimental import pallas as pl
from jax.experimental.pallas import tpu as pltpu

F32 = jnp.float32
BF16 = jnp.bfloat16

D_MODEL = 1024
ATTN_HEADS = 8
ATTN_KV_HEADS = 2
ATTN_GROUP = ATTN_HEADS // ATTN_KV_HEADS
HEAD_DIM = 64
WINDOW = 128
BLOCK = 128
NUM_BUCKETS = 32
MAX_DISTANCE = 128
GLA_HEADS = 4
GLA_KEY_DIM = D_MODEL // 2
GLA_VAL_DIM = D_MODEL
GLA_HEAD_K = GLA_KEY_DIM // GLA_HEADS
GLA_HEAD_V = GLA_VAL_DIM // GLA_HEADS
GLA_GATE_RANK = 16
GLA_GATE_TAU = 16.0
D_FF = -(-8 * D_MODEL // (3 * 256)) * 256
PLE_DIM = 256
EPS = 1e-6
NEG = -1e30
LOG2E = math.log2(math.e)

LANES = 128
SUBLANES = 8

TM_MIX = 512
GLA_CHUNK = 128
GLA_LEVELS = 7
TM_OUT = 512
FFN_COLS = 256
VMEM_LIMIT = 56 * 1024 * 1024

N_QA = ATTN_HEADS * HEAD_DIM
N_QKVA = 768
SEG_QKG = (768, 1792)
SEG_VG = (1792, 2816)
GK0 = 2816
N_ACT = 3072

NT = (((1,), (1,)), ((), ()))


def _sigmoid(x):
    return 0.5 * jnp.tanh(0.5 * x) + 0.5


def _rms(x, gain):
    ms = jnp.mean(x * x, axis=-1, keepdims=True)
    return x * lax.rsqrt(ms + EPS) * gain


def _chunk_cumsum(la):
    tm, n = la.shape
    groups = tm // SUBLANES
    x = la.reshape(groups, SUBLANES, n)
    sub = lax.broadcasted_iota(jnp.int32, x.shape, 1)
    for s in (1, 2, 4):
        x = x + jnp.where(sub >= s, pltpu.roll(x, s, 1), 0.0)
    per_chunk = GLA_CHUNK // SUBLANES
    out = []
    carry = None
    for i in range(groups):
        blk = x[i]
        if i % per_chunk:
            blk = blk + carry
        out.append(blk)
        carry = jnp.broadcast_to(blk[SUBLANES - 1:SUBLANES, :], (SUBLANES, n))
    return jnp.concatenate(out, axis=0)


def _attn_probs(j, first, q_scr, kv_scr, bias_ref):
    r0 = j * BLOCK
    band_row = lax.broadcasted_iota(jnp.int32, (2 * BLOCK, LANES), 0)
    low = lax.broadcasted_iota(jnp.int32, (2 * BLOCK, LANES), 1) < HEAD_DIM
    ones = jnp.ones((2 * BLOCK, LANES), BF16)
    kband = kv_scr[r0:r0 + 2 * BLOCK, 0:LANES]
    vband = kv_scr[r0:r0 + 2 * BLOCK, LANES:2 * LANES]
    zero = jnp.zeros_like(kband)
    kband = jnp.where(band_row == 0, zero, kband)
    vband = jnp.where(band_row == 0, zero, vband)
    kroll = pltpu.roll(kband, HEAD_DIM, 1)
    vroll = pltpu.roll(vband, HEAD_DIM, 1)
    ktile = ((jnp.where(low, kband, zero), jnp.where(low, zero, kroll)),
             (jnp.where(low, kroll, zero), jnp.where(low, zero, kband)))
    if j == 0:
        col = lax.broadcasted_iota(jnp.int32, (BLOCK, 2 * BLOCK), 1)
        no_prev = jnp.logical_and(first, jnp.logical_and(col >= 1, col < BLOCK))
    probs = [[None, None], [None, None]]
    for kv in range(ATTN_KV_HEADS):
        qg = q_scr[r0:r0 + BLOCK, 2 * kv * LANES:(2 * kv + 2) * LANES]
        qg = jnp.concatenate([qg[:, :LANES], qg[:, LANES:]], axis=0)
        for odd in range(2):
            s2 = lax.dot_general(qg, ktile[kv][odd], NT, preferred_element_type=F32)
            ps = []
            for half in range(2):
                head = 2 * (2 * kv + half) + odd
                s = s2[half * BLOCK:(half + 1) * BLOCK] + bias_ref[head]
                if j == 0:
                    s = jnp.where(no_prev, NEG, s)
                ps.append(jnp.exp(s - jnp.max(s, axis=-1, keepdims=True)).astype(BF16))
            probs[kv][odd] = jnp.concatenate(ps, axis=0)
    vnat = jnp.concatenate([vband, ones], axis=1)
    vrol = jnp.concatenate([vroll, ones], axis=1)
    p_nat = jnp.concatenate([probs[0][0], probs[1][1]], axis=0)
    p_rol = jnp.concatenate([probs[0][1], probs[1][0]], axis=0)
    return p_nat, vnat, p_rol, vrol


def _attn_out(j, p_nat, vnat, p_rol, vrol, oa_ref):
    r0 = j * BLOCK
    low_o = lax.broadcasted_iota(jnp.int32, (BLOCK, LANES), 1) < HEAD_DIM
    od_nat = jnp.dot(p_nat, vnat, preferred_element_type=F32)
    od_rol = jnp.dot(p_rol, vrol, preferred_element_type=F32)
    for blk in range(ATTN_HEADS // 2):
        rows = slice(blk * BLOCK, (blk + 1) * BLOCK)
        o_nat = od_nat[rows, :LANES] / od_nat[rows, LANES:]
        o_rol = od_rol[rows, :LANES] / od_rol[rows, LANES:]
        even, odd_ = (o_nat, o_rol) if blk < ATTN_HEADS // 4 else (o_rol, o_nat)
        oa_ref[r0:r0 + BLOCK, blk * LANES:(blk + 1) * LANES] = jnp.where(low_o, even, odd_).astype(BF16)


def _anchor_small(grow, b):
    C = GLA_CHUNK

    def row(r):
        return jnp.broadcast_to(grow(r), (SUBLANES, LANES))

    sub = lax.broadcasted_iota(jnp.int32, (SUBLANES, LANES), 0)
    parts = []
    for i in range(C // SUBLANES):
        r0 = i * SUBLANES
        if b == 4:
            a = row(r0 + 3)
        elif b == 2:
            a = jnp.where(sub < 4, row(r0 + 1), row(r0 + 5))
        else:
            a = jnp.where(sub < 2, row(r0), jnp.where(sub < 4, row(r0 + 2),
                                                      jnp.where(sub < 6, row(r0 + 4), row(r0 + 6))))
        parts.append(a)
    return jnp.concatenate(parts, axis=0)


def _level_operand(q, k, g, grow, b, row):
    C = GLA_CHUNK
    if b >= SUBLANES:
        expo, src = [], []
        for p in range(C // (2 * b)):
            lo, mid, hi = 2 * b * p, 2 * b * p + b, 2 * b * (p + 1)
            a = jnp.broadcast_to(grow(mid - 1), (b, LANES))
            expo += [a - g[lo:mid], g[mid:hi] - a]
            src += [k[lo:mid], q[mid:hi]]
        expo = jnp.concatenate(expo, axis=0)
        src = jnp.concatenate(src, axis=0)
    else:
        upper = (row & b) != 0
        d = g - _anchor_small(grow, b)
        expo = jnp.where(upper, d, -d)
        src = jnp.where(upper, q, k)
    return (src * jnp.exp2(expo)).astype(BF16)


def _gla_scores(c, heads, qkg_scr, vg_scr, g_scr, lv):
    C = GLA_CHUNK
    r0 = c * C
    row = lax.broadcasted_iota(jnp.int32, (C, LANES), 0)
    per_head = []
    for h in heads:
        cols = slice(h * GLA_HEAD_K, (h + 1) * GLA_HEAD_K)
        grow = lambda r, cols=cols: g_scr[r0 + r:r0 + r + 1, cols]
        q_b = qkg_scr[r0:r0 + C, cols]
        k_b = qkg_scr[r0:r0 + C, GLA_KEY_DIM + h * GLA_HEAD_K:GLA_KEY_DIM + (h + 1) * GLA_HEAD_K]
        q = q_b.astype(F32)
        k = k_b.astype(F32)
        v = vg_scr[r0:r0 + C, h * GLA_HEAD_V:(h + 1) * GLA_HEAD_V]
        g = g_scr[r0:r0 + C, cols]
        glast = jnp.broadcast_to(grow(C - 1), (C, GLA_HEAD_K))

        scores = jnp.where(lv == GLA_LEVELS, lax.dot_general(q_b, k_b, NT, preferred_element_type=F32), 0.0)
        for l in range(GLA_LEVELS):
            r = _level_operand(q, k, g, grow, 1 << l, row)
            scores = jnp.where(lv == l, lax.dot_general(r, r, NT, preferred_element_type=F32), scores)
        per_head.append((h, q, k, v, g, glast, scores))
    return per_head


def _gla_state(c, per_head, s_scr, og_ref):
    C = GLA_CHUNK
    r0 = c * C
    for h, q, k, v, g, glast, scores in per_head:
        state = s_scr[h]
        qt = (q * jnp.exp2(g)).astype(BF16)
        lhs = jnp.concatenate([qt, scores.astype(BF16)], axis=1)
        rhs = jnp.concatenate([state.astype(BF16), v], axis=0)
        o = jnp.dot(lhs, rhs, preferred_element_type=F32)
        og_ref[r0:r0 + C, h * GLA_HEAD_V:(h + 1) * GLA_HEAD_V] = o.astype(BF16)

        kd_t = (k * jnp.exp2(glast - g)).T.astype(BF16)
        gam = jnp.exp2(glast.T)
        upd = jnp.dot(kd_t, v, preferred_element_type=F32)
        s_scr[h] = state * jnp.concatenate([gam, gam], axis=1) + upd


def _mixer_body(tiles_per_seq, x_ref, nm_ref, wf_ref, wa_ref, wgk2_ref, bgk_ref, bias_ref, lv_ref,
                oa_ref, og_ref, gate_ref, q_scr, kv_scr, qkg_scr, vg_scr, g_scr, s_scr):
    i = pl.program_id(0)
    first = (i % tiles_per_seq) == 0

    @pl.when(i == 0)
    def _():
        kv_scr[0:BLOCK, :] = jnp.zeros((BLOCK, 2 * LANES), BF16)

    @pl.when(first)
    def _():
        s_scr[...] = jnp.zeros_like(s_scr)

    h = _rms(x_ref[...], nm_ref[...]).astype(BF16)

    def proj(w_ref, c0, c1):
        return jnp.dot(h, w_ref[:, c0:c1], preferred_element_type=F32)

    w_kvgk = jnp.concatenate(
        [wf_ref[:, N_QA:N_QKVA], wf_ref[:, GK0:GK0 + LANES], jnp.zeros((D_MODEL, LANES), BF16)], axis=1)
    kvgk = jnp.dot(h, w_kvgk, preferred_element_type=F32)
    kv_scr[BLOCK:BLOCK + TM_MIX, :] = kvgk[:, :2 * LANES].astype(BF16)
    gk = kvgk[:, 2 * LANES:3 * LANES].astype(BF16)

    def decay_chunk(j):
        rows = slice(j * GLA_CHUNK, (j + 1) * GLA_CHUNK)
        z = jnp.dot(gk[rows], wgk2_ref[...], preferred_element_type=F32) + bgk_ref[...]
        la = (jnp.minimum(z, 0.0) - jnp.log(1.0 + jnp.exp(-jnp.abs(z)))) * (LOG2E / GLA_GATE_TAU)
        g_scr[rows, :] = _chunk_cumsum(la)

    q_scr[...] = (proj(wf_ref, 0, N_QA) * (HEAD_DIM ** -0.5)).astype(BF16)

    def gla_proj(j):
        if j < 2:
            a = SEG_QKG[0] + 512 * j
            r = proj(wf_ref, a, a + 512)
            if j == 0:
                r = r * (GLA_HEAD_K ** -0.5)
            qkg_scr[:, 512 * j:512 * (j + 1)] = r.astype(BF16)
        else:
            a = SEG_VG[0] + 512 * (j - 2)
            vg_scr[:, 512 * (j - 2):512 * (j - 1)] = proj(wf_ref, a, a + 512).astype(BF16)

    for j in range(TM_MIX // BLOCK):
        att = _attn_probs(j, first, q_scr, kv_scr, bias_ref)
        gla_proj(j)
        decay_chunk(j)
        _attn_out(j, *att, oa_ref)

    def act_cols(c):
        r = proj(wa_ref, 512 * c, 512 * (c + 1))
        t = jnp.tanh(r)
        s = r * (t + 1.0) if c < 2 else 0.5 * t + 0.5
        gate_ref[:, 512 * c:512 * (c + 1)] = s.astype(BF16)

    lv = lv_ref[...]
    n_act = N_ACT // 512
    chunks = TM_MIX // GLA_CHUNK
    act_of_chunk = [list(range(n_act))[j::chunks] for j in range(chunks)]
    for j in range(chunks):
        acts = list(act_of_chunk[j])
        lo = _gla_scores(j, (0, 1), qkg_scr, vg_scr, g_scr, lv)
        act_cols(acts.pop(0))
        hi = _gla_scores(j, (2, 3), qkg_scr, vg_scr, g_scr, lv)
        _gla_state(j, lo, s_scr, og_ref)
        for c in acts:
            act_cols(c)
        _gla_state(j, hi, s_scr, og_ref)

    kv_scr[0:BLOCK, :] = kv_scr[TM_MIX:TM_MIX + BLOCK, :]


def _mixer_call(x2, seq_len, nm, w_all, w_act, w_gk2, b_gk, bias, lv):
    T = x2.shape[0]
    tm = TM_MIX
    full = lambda a: pl.BlockSpec(a.shape, lambda i: (0,) * a.ndim)
    rows = lambda n: pl.BlockSpec((tm, n), lambda i: (i, 0))
    front = pl.BlockSpec((D_MODEL, GK0 + LANES), lambda i: (0, 0))
    consts = [nm, w_all, w_act, w_gk2, b_gk, bias, lv]
    return pl.pallas_call(
        partial(_mixer_body, seq_len // tm),
        grid=(T // tm,),
        in_specs=[rows(D_MODEL), full(nm), front] + [full(a) for a in consts[2:]],
        out_specs=[rows(N_QA), rows(GLA_VAL_DIM), rows(N_ACT)],
        out_shape=[
            jax.ShapeDtypeStruct((T, N_QA), BF16),
            jax.ShapeDtypeStruct((T, GLA_VAL_DIM), BF16),
            jax.ShapeDtypeStruct((T, N_ACT), BF16),
        ],
        scratch_shapes=[
            pltpu.VMEM((tm, N_QA), BF16),
            pltpu.VMEM((BLOCK + tm, 2 * LANES), BF16),
            pltpu.VMEM((tm, 2 * GLA_KEY_DIM), BF16),
            pltpu.VMEM((tm, GLA_VAL_DIM), BF16),
            pltpu.VMEM((tm, GLA_KEY_DIM), F32),
            pltpu.VMEM((GLA_HEADS, GLA_HEAD_K, GLA_HEAD_V), F32),
        ],
        compiler_params=pltpu.CompilerParams(
            dimension_semantics=("arbitrary",), vmem_limit_bytes=VMEM_LIMIT),
        name="mixers",
    )(x2, *consts)


def _out_body(x_ref, p_ref, oa_ref, og_ref, gate_ref,
              wpa_ref, wpg_ref, wo_ref, wfi_ref, wfo_ref, wpgate_ref, wple_ref,
              gn_ref, nffn_ref, nple_ref, nfin_ref, out_ref, a_scr):
    halves = [slice(0, TM_OUT // 2), slice(TM_OUT // 2, TM_OUT)]
    w = FFN_COLS

    x1 = []
    for r in halves:
        ya = jnp.dot(oa_ref[r, :], wpa_ref[...], preferred_element_type=F32)
        parts = []
        for h in range(GLA_HEADS):
            cols = slice(h * GLA_HEAD_V, (h + 1) * GLA_HEAD_V)
            oh = _rms(og_ref[r, cols].astype(F32), gn_ref[...]) * gate_ref[r, cols].astype(F32)
            parts.append(oh.astype(BF16))
        yg = jnp.dot(jnp.concatenate(parts, axis=1), wpg_ref[...], preferred_element_type=F32)
        mixed = (gate_ref[r, GLA_VAL_DIM:GLA_VAL_DIM + D_MODEL].astype(F32) * ya
                 + gate_ref[r, GLA_VAL_DIM + D_MODEL:N_ACT].astype(F32) * yg)
        x1.append(x_ref[r, :] + jnp.dot(mixed.astype(BF16), wo_ref[...], preferred_element_type=F32))

    h2 = [_rms(v, nffn_ref[...]).astype(BF16) for v in x1]
    for c in range(D_FF // w):
        w_gu = jnp.concatenate(
            [wfi_ref[:, w * c:w * (c + 1)], wfi_ref[:, D_FF + w * c:D_FF + w * (c + 1)]], axis=1)
        for r, hh in zip(halves, h2):
            gu = jnp.dot(hh, w_gu, preferred_element_type=F32)
            gg = gu[:, :w]
            a_scr[r, w * c:w * (c + 1)] = (gg * _sigmoid(gg) * gu[:, w:]).astype(BF16)
    x2 = [v + jnp.dot(a_scr[r, :], wfo_ref[...], preferred_element_type=F32) for r, v in zip(halves, x1)]

    h3 = [_rms(v, nple_ref[...]).astype(BF16) for v in x2]
    for r, hh, v in zip(halves, h3, x2):
        gate = _sigmoid(jnp.dot(hh, wpgate_ref[...], preferred_element_type=F32))
        pe = jnp.dot(p_ref[r, :].astype(BF16), wple_ref[...], preferred_element_type=F32)
        out_ref[r, :] = _rms(v + gate * pe, nfin_ref[...])


def _out_call(x2, p2, oa, og, act, weights, gains):
    T = x2.shape[0]
    tm = TM_OUT
    full = lambda a: pl.BlockSpec(a.shape, lambda i: (0,) * a.ndim)
    rows = lambda n: pl.BlockSpec((tm, n), lambda i: (i, 0))
    return pl.pallas_call(
        _out_body,
        grid=(T // tm,),
        in_specs=[rows(D_MODEL), rows(PLE_DIM), rows(N_QA), rows(GLA_VAL_DIM), rows(N_ACT)]
                 + [full(w) for w in weights] + [full(gn) for gn in gains],
        out_specs=rows(D_MODEL),
        out_shape=jax.ShapeDtypeStruct((T, D_MODEL), F32),
        scratch_shapes=[pltpu.VMEM((tm, D_FF), BF16)],
        compiler_params=pltpu.CompilerParams(
            dimension_semantics=("arbitrary",), vmem_limit_bytes=VMEM_LIMIT),
        name="merge_ffn",
    )(x2, p2, oa, og, act, *weights, *gains)


def _t5_bucket_table():
    q_loc = np.arange(BLOCK)[:, None] + BLOCK
    k_loc = np.arange(2 * BLOCK)[None, :]
    dist = np.maximum(q_loc - k_loc, 0)
    max_exact = NUM_BUCKETS // 2
    d_f = np.maximum(dist, max_exact).astype(np.float32)
    large = max_exact + (np.log(d_f / max_exact) / math.log(MAX_DISTANCE / max_exact)
                         * (NUM_BUCKETS - max_exact)).astype(np.int32)
    large = np.minimum(large, NUM_BUCKETS - 1)
    bucket = np.where(dist < max_exact, dist, large)
    raw = q_loc - k_loc
    in_window = (raw >= 0) & (raw < WINDOW)
    return bucket.astype(np.int32), in_window


def _level_table():
    r = np.arange(GLA_CHUNK)[:, None]
    c = np.arange(GLA_CHUNK)[None, :]
    x = np.bitwise_xor(r, c)
    lv = np.where(x > 0, np.floor(np.log2(np.maximum(x, 1))).astype(np.int32), GLA_LEVELS)
    return np.where(r >= c, lv, -1).astype(np.int32)


def _attn_bias(rel_table, sinks):
    bucket, in_window = _t5_bucket_table()
    bucket = jnp.asarray(bucket)
    rel = rel_table.astype(F32)
    bias = jnp.zeros((ATTN_HEADS, BLOCK, 2 * BLOCK), F32)
    for b in range(NUM_BUCKETS):
        bias = jnp.where(bucket[None] == b, rel[b][:, None, None], bias)
    bias = jnp.where(jnp.asarray(in_window)[None], bias, NEG)
    col0 = jnp.asarray(np.arange(2 * BLOCK) == 0)[None, None, :]
    return jnp.where(col0, sinks.astype(F32)[:, None, None], bias)


def kernel(x, p, w_in, w_gk2, b_gk, sinks, rel_table, w_proj_attn, w_proj_gla, gla_norm, w_out,
           norm_mix, norm_ffn, w_ffn_in, w_ffn_out, norm_ple, w_ple_gate, w_ple, norm_final):
    B, S, D = x.shape
    T = B * S
    assert w_in.shape[0] == 1, "single-layer trunk only"
    assert S % TM_MIX == 0 and T % TM_OUT == 0
    i = 0
    lv = jnp.asarray(_level_table())
    bias = _attn_bias(rel_table, sinks[i])

    xs = x.reshape(T, D)
    w_all = w_in[i].astype(BF16)
    w_act = w_all[:, GK0 + GLA_GATE_RANK:] * 0.5
    w_gk2p = jnp.pad(w_gk2[i], ((0, LANES - GLA_GATE_RANK), (0, 0))).astype(BF16)

    oa, og, act = _mixer_call(
        xs, S, norm_mix[i].reshape(1, D), w_all, w_act, w_gk2p, b_gk[i].reshape(1, -1), bias, lv)

    weights = [
        w_proj_attn[i].astype(BF16), w_proj_gla[i].astype(BF16), w_out[i].astype(BF16),
        w_ffn_in[i].astype(BF16), w_ffn_out[i].astype(BF16), w_ple_gate[i].astype(BF16), w_ple[i].astype(BF16),
    ]
    gains = [gla_norm[i].reshape(1, -1), norm_ffn[i].reshape(1, D), norm_ple[i].reshape(1, D),
             norm_final.reshape(1, D)]
    out = _out_call(xs, p[i].reshape(T, PLE_DIM), oa, og, act, weights, gains)
    return out.reshape(B, S, D)
```

```python
import math
from functools import partial

import numpy as np
import jax
import jax.numpy as jnp
from jax import lax
from jax.experimental import pallas as pl
from jax.experimental.pallas import tpu as pltpu

F32 = jnp.float32
BF16 = jnp.bfloat16

D_MODEL = 1024
ATTN_HEADS = 8
ATTN_KV_HEADS = 2
ATTN_GROUP = ATTN_HEADS // ATTN_KV_HEADS
HEAD_DIM = 64
WINDOW = 128
BLOCK = 128
NUM_BUCKETS = 32
MAX_DISTANCE = 128
GLA_HEADS = 4
GLA_KEY_DIM = D_MODEL // 2
GLA_VAL_DIM = D_MODEL
GLA_HEAD_K = GLA_KEY_DIM // GLA_HEADS
GLA_HEAD_V = GLA_VAL_DIM // GLA_HEADS
GLA_GATE_RANK = 16
GLA_GATE_TAU = 16.0
D_FF = -(-8 * D_MODEL // (3 * 256)) * 256
PLE_DIM = 256
EPS = 1e-6
NEG = -1e30
LOG2E = math.log2(math.e)

LANES = 128
SUBLANES = 8

TM_MIX = 512
GLA_CHUNK = 128
GLA_LEVELS = 7
TM_OUT = 512
FFN_COLS = 256
VMEM_LIMIT = 56 * 1024 * 1024

N_QA = ATTN_HEADS * HEAD_DIM
N_QKVA = 768
SEG_QKG = (768, 1792)
SEG_VG = (1792, 2816)
GK0 = 2816
N_ACT = 3072

NT = (((1,), (1,)), ((), ()))


def _sigmoid(x):
    return 0.5 * jnp.tanh(0.5 * x) + 0.5


def _rms(x, gain):
    ms = jnp.mean(x * x, axis=-1, keepdims=True)
    return x * lax.rsqrt(ms + EPS) * gain


def _chunk_cumsum(la):
    tm, n = la.shape
    groups = tm // SUBLANES
    x = la.reshape(groups, SUBLANES, n)
    sub = lax.broadcasted_iota(jnp.int32, x.shape, 1)
    for s in (1, 2, 4):
        x = x + jnp.where(sub >= s, pltpu.roll(x, s, 1), 0.0)
    per_chunk = GLA_CHUNK // SUBLANES
    out = []
    carry = None
    for i in range(groups):
        blk = x[i]
        if i % per_chunk:
            blk = blk + carry
        out.append(blk)
        carry = jnp.broadcast_to(blk[SUBLANES - 1:SUBLANES, :], (SUBLANES, n))
    return jnp.concatenate(out, axis=0)


def _attn_probs(j, first, q_scr, kv_scr, bias_ref):
    r0 = j * BLOCK
    band_row = lax.broadcasted_iota(jnp.int32, (2 * BLOCK, LANES), 0)
    low = lax.broadcasted_iota(jnp.int32, (2 * BLOCK, LANES), 1) < HEAD_DIM
    ones = jnp.ones((2 * BLOCK, LANES), BF16)
    kband = kv_scr[r0:r0 + 2 * BLOCK, 0:LANES]
    vband = kv_scr[r0:r0 + 2 * BLOCK, LANES:2 * LANES]
    zero = jnp.zeros_like(kband)
    kband = jnp.where(band_row == 0, zero, kband)
    vband = jnp.where(band_row == 0, zero, vband)
    kroll = pltpu.roll(kband, HEAD_DIM, 1)
    vroll = pltpu.roll(vband, HEAD_DIM, 1)
    ktile = ((jnp.where(low, kband, zero), jnp.where(low, zero, kroll)),
             (jnp.where(low, kroll, zero), jnp.where(low, zero, kband)))
    if j == 0:
        col = lax.broadcasted_iota(jnp.int32, (BLOCK, 2 * BLOCK), 1)
        no_prev = jnp.logical_and(first, jnp.logical_and(col >= 1, col < BLOCK))
    probs = [[None, None], [None, None]]
    for kv in range(ATTN_KV_HEADS):
        qg = q_scr[r0:r0 + BLOCK, 2 * kv * LANES:(2 * kv + 2) * LANES]
        qg = jnp.concatenate([qg[:, :LANES], qg[:, LANES:]], axis=0)
        for odd in range(2):
            s2 = lax.dot_general(qg, ktile[kv][odd], NT, preferred_element_type=F32)
            ps = []
            for half in range(2):
                head = 2 * (2 * kv + half) + odd
                s = s2[half * BLOCK:(half + 1) * BLOCK] + bias_ref[head]
                if j == 0:
                    s = jnp.where(no_prev, NEG, s)
                ps.append(jnp.exp(s - jnp.max(s, axis=-1, keepdims=True)).astype(BF16))
            probs[kv][odd] = jnp.concatenate(ps, axis=0)
    vnat = jnp.concatenate([vband, ones], axis=1)
    vrol = jnp.concatenate([vroll, ones], axis=1)
    p_nat = jnp.concatenate([probs[0][0], probs[1][1]], axis=0)
    p_rol = jnp.concatenate([probs[0][1], probs[1][0]], axis=0)
    return p_nat, vnat, p_rol, vrol


def _attn_out(j, p_nat, vnat, p_rol, vrol, oa_ref):
    r0 = j * BLOCK
    low_o = lax.broadcasted_iota(jnp.int32, (BLOCK, LANES), 1) < HEAD_DIM
    od_nat = jnp.dot(p_nat, vnat, preferred_element_type=F32)
    od_rol = jnp.dot(p_rol, vrol, preferred_element_type=F32)
    for blk in range(ATTN_HEADS // 2):
        rows = slice(blk * BLOCK, (blk + 1) * BLOCK)
        o_nat = od_nat[rows, :LANES] / od_nat[rows, LANES:]
        o_rol = od_rol[rows, :LANES] / od_rol[rows, LANES:]
        even, odd_ = (o_nat, o_rol) if blk < ATTN_HEADS // 4 else (o_rol, o_nat)
        oa_ref[r0:r0 + BLOCK, blk * LANES:(blk + 1) * LANES] = jnp.where(low_o, even, odd_).astype(BF16)


def _anchor_small(grow, b):
    C = GLA_CHUNK

    def row(r):
        return jnp.broadcast_to(grow(r), (SUBLANES, LANES))

    sub = lax.broadcasted_iota(jnp.int32, (SUBLANES, LANES), 0)
    parts = []
    for i in range(C // SUBLANES):
        r0 = i * SUBLANES
        if b == 4:
            a = row(r0 + 3)
        elif b == 2:
            a = jnp.where(sub < 4, row(r0 + 1), row(r0 + 5))
        else:
            a = jnp.where(sub < 2, row(r0), jnp.where(sub < 4, row(r0 + 2),
                                                      jnp.where(sub < 6, row(r0 + 4), row(r0 + 6))))
        parts.append(a)
    return jnp.concatenate(parts, axis=0)


def _level_operand(q, k, g, grow, b, row):
    C = GLA_CHUNK
    if b >= SUBLANES:
        expo, src = [], []
        for p in range(C // (2 * b)):
            lo, mid, hi = 2 * b * p, 2 * b * p + b, 2 * b * (p + 1)
            a = jnp.broadcast_to(grow(mid - 1), (b, LANES))
            expo += [a - g[lo:mid], g[mid:hi] - a]
            src += [k[lo:mid], q[mid:hi]]
        expo = jnp.concatenate(expo, axis=0)
        src = jnp.concatenate(src, axis=0)
    else:
        upper = (row & b) != 0
        d = g - _anchor_small(grow, b)
        expo = jnp.where(upper, d, -d)
        src = jnp.where(upper, q, k)
    r = src * jnp.exp2(expo)
    return r.astype(BF16), r.T.astype(BF16)


def _gla_scores(c, heads, qkg_scr, vg_scr, g_scr, lv):
    C = GLA_CHUNK
    r0 = c * C
    row = lax.broadcasted_iota(jnp.int32, (C, LANES), 0)
    per_head = []
    for h in heads:
        cols = slice(h * GLA_HEAD_K, (h + 1) * GLA_HEAD_K)
        grow = lambda r, cols=cols: g_scr[r0 + r:r0 + r + 1, cols]
        q_b = qkg_scr[r0:r0 + C, cols]
        k_b = qkg_scr[r0:r0 + C, GLA_KEY_DIM + h * GLA_HEAD_K:GLA_KEY_DIM + (h + 1) * GLA_HEAD_K]
        q = q_b.astype(F32)
        k = k_b.astype(F32)
        v = vg_scr[r0:r0 + C, h * GLA_HEAD_V:(h + 1) * GLA_HEAD_V]
        g = g_scr[r0:r0 + C, cols]
        glast = jnp.broadcast_to(grow(C - 1), (C, GLA_HEAD_K))

        scores = jnp.where(lv == GLA_LEVELS, jnp.dot(q_b, k.T.astype(BF16), preferred_element_type=F32), 0.0)
        for l in range(GLA_LEVELS):
            r, r_t = _level_operand(q, k, g, grow, 1 << l, row)
            scores = jnp.where(lv == l, jnp.dot(r, r_t, preferred_element_type=F32), scores)
        per_head.append((h, q, k, v, g, glast, scores))
    return per_head


def _gla_state(c, per_head, s_scr, og_ref):
    C = GLA_CHUNK
    r0 = c * C
    for h, q, k, v, g, glast, scores in per_head:
        state = s_scr[h]
        qt = (q * jnp.exp2(g)).astype(BF16)
        lhs = jnp.concatenate([qt, scores.astype(BF16)], axis=1)
        rhs = jnp.concatenate([state.astype(BF16), v], axis=0)
        o = jnp.dot(lhs, rhs, preferred_element_type=F32)
        og_ref[r0:r0 + C, h * GLA_HEAD_V:(h + 1) * GLA_HEAD_V] = o.astype(BF16)

        kd_t = (k * jnp.exp2(glast - g)).T.astype(BF16)
        gam = jnp.exp2(glast.T)
        upd = jnp.dot(kd_t, v, preferred_element_type=F32)
        s_scr[h] = state * jnp.concatenate([gam, gam], axis=1) + upd


def _mixer_body(tiles_per_seq, x_ref, nm_ref, wf_ref, wa_ref, wgk2_ref, bgk_ref, bias_ref, lv_ref,
                oa_ref, og_ref, gate_ref, q_scr, kv_scr, qkg_scr, vg_scr, g_scr, s_scr):
    i = pl.program_id(0)
    first = (i % tiles_per_seq) == 0

    @pl.when(i == 0)
    def _():
        kv_scr[0:BLOCK, :] = jnp.zeros((BLOCK, 2 * LANES), BF16)

    @pl.when(first)
    def _():
        s_scr[...] = jnp.zeros_like(s_scr)

    h = _rms(x_ref[...], nm_ref[...]).astype(BF16)

    def proj(w_ref, c0, c1):
        return jnp.dot(h, w_ref[:, c0:c1], preferred_element_type=F32)

    w_kvgk = jnp.concatenate(
        [wf_ref[:, N_QA:N_QKVA], wf_ref[:, GK0:GK0 + LANES], jnp.zeros((D_MODEL, LANES), BF16)], axis=1)
    kvgk = jnp.dot(h, w_kvgk, preferred_element_type=F32)
    kv_scr[BLOCK:BLOCK + TM_MIX, :] = kvgk[:, :2 * LANES].astype(BF16)
    gk = kvgk[:, 2 * LANES:3 * LANES].astype(BF16)

    def decay_chunk(j):
        rows = slice(j * GLA_CHUNK, (j + 1) * GLA_CHUNK)
        z = jnp.dot(gk[rows], wgk2_ref[...], preferred_element_type=F32) + bgk_ref[...]
        la = (jnp.minimum(z, 0.0) - jnp.log(1.0 + jnp.exp(-jnp.abs(z)))) * (LOG2E / GLA_GATE_TAU)
        g_scr[rows, :] = _chunk_cumsum(la)

    q_scr[...] = (proj(wf_ref, 0, N_QA) * (HEAD_DIM ** -0.5)).astype(BF16)

    def gla_proj(j):
        if j < 2:
            a = SEG_QKG[0] + 512 * j
            r = proj(wf_ref, a, a + 512)
            if j == 0:
                r = r * (GLA_HEAD_K ** -0.5)
            qkg_scr[:, 512 * j:512 * (j + 1)] = r.astype(BF16)
        else:
            a = SEG_VG[0] + 512 * (j - 2)
            vg_scr[:, 512 * (j - 2):512 * (j - 1)] = proj(wf_ref, a, a + 512).astype(BF16)

    for j in range(TM_MIX // BLOCK):
        att = _attn_probs(j, first, q_scr, kv_scr, bias_ref)
        gla_proj(j)
        decay_chunk(j)
        _attn_out(j, *att, oa_ref)

    def act_cols(c):
        r = proj(wa_ref, 512 * c, 512 * (c + 1))
        t = jnp.tanh(r)
        s = r * (t + 1.0) if c < 2 else 0.5 * t + 0.5
        gate_ref[:, 512 * c:512 * (c + 1)] = s.astype(BF16)

    lv = lv_ref[...]
    n_act = N_ACT // 512
    chunks = TM_MIX // GLA_CHUNK
    act_of_chunk = [list(range(n_act))[j::chunks] for j in range(chunks)]
    for j in range(chunks):
        acts = list(act_of_chunk[j])
        lo = _gla_scores(j, (0, 1), qkg_scr, vg_scr, g_scr, lv)
        act_cols(acts.pop(0))
        hi = _gla_scores(j, (2, 3), qkg_scr, vg_scr, g_scr, lv)
        _gla_state(j, lo, s_scr, og_ref)
        for c in acts:
            act_cols(c)
        _gla_state(j, hi, s_scr, og_ref)

    kv_scr[0:BLOCK, :] = kv_scr[TM_MIX:TM_MIX + BLOCK, :]


def _mixer_call(x2, seq_len, nm, w_all, w_act, w_gk2, b_gk, bias, lv):
    T = x2.shape[0]
    tm = TM_MIX
    full = lambda a: pl.BlockSpec(a.shape, lambda i: (0,) * a.ndim)
    rows = lambda n: pl.BlockSpec((tm, n), lambda i: (i, 0))
    front = pl.BlockSpec((D_MODEL, GK0 + LANES), lambda i: (0, 0))
    consts = [nm, w_all, w_act, w_gk2, b_gk, bias, lv]
    return pl.pallas_call(
        partial(_mixer_body, seq_len // tm),
        grid=(T // tm,),
        in_specs=[rows(D_MODEL), full(nm), front] + [full(a) for a in consts[2:]],
        out_specs=[rows(N_QA), rows(GLA_VAL_DIM), rows(N_ACT)],
        out_shape=[
            jax.ShapeDtypeStruct((T, N_QA), BF16),
            jax.ShapeDtypeStruct((T, GLA_VAL_DIM), BF16),
            jax.ShapeDtypeStruct((T, N_ACT), BF16),
        ],
        scratch_shapes=[
            pltpu.VMEM((tm, N_QA), BF16),
            pltpu.VMEM((BLOCK + tm, 2 * LANES), BF16),
            pltpu.VMEM((tm, 2 * GLA_KEY_DIM), BF16),
            pltpu.VMEM((tm, GLA_VAL_DIM), BF16),
            pltpu.VMEM((tm, GLA_KEY_DIM), F32),
            pltpu.VMEM((GLA_HEADS, GLA_HEAD_K, GLA_HEAD_V), F32),
        ],
        compiler_params=pltpu.CompilerParams(
            dimension_semantics=("arbitrary",), vmem_limit_bytes=VMEM_LIMIT),
        name="mixers",
    )(x2, *consts)


def _out_body(x_ref, p_ref, oa_ref, og_ref, gate_ref,
              wpa_ref, wpg_ref, wo_ref, wfi_ref, wfo_ref, wpgate_ref, wple_ref,
              gn_ref, nffn_ref, nple_ref, nfin_ref, out_ref, a_scr):
    halves = [slice(0, TM_OUT // 2), slice(TM_OUT // 2, TM_OUT)]
    w = FFN_COLS

    x1 = []
    for r in halves:
        ya = jnp.dot(oa_ref[r, :], wpa_ref[...], preferred_element_type=F32)
        parts = []
        for h in range(GLA_HEADS):
            cols = slice(h * GLA_HEAD_V, (h + 1) * GLA_HEAD_V)
            oh = _rms(og_ref[r, cols].astype(F32), gn_ref[...]) * gate_ref[r, cols].astype(F32)
            parts.append(oh.astype(BF16))
        yg = jnp.dot(jnp.concatenate(parts, axis=1), wpg_ref[...], preferred_element_type=F32)
        mixed = (gate_ref[r, GLA_VAL_DIM:GLA_VAL_DIM + D_MODEL].astype(F32) * ya
                 + gate_ref[r, GLA_VAL_DIM + D_MODEL:N_ACT].astype(F32) * yg)
        x1.append(x_ref[r, :] + jnp.dot(mixed.astype(BF16), wo_ref[...], preferred_element_type=F32))

    h2 = [_rms(v, nffn_ref[...]).astype(BF16) for v in x1]
    for c in range(D_FF // w):
        w_gu = jnp.concatenate(
            [wfi_ref[:, w * c:w * (c + 1)], wfi_ref[:, D_FF + w * c:D_FF + w * (c + 1)]], axis=1)
        for r, hh in zip(halves, h2):
            gu = jnp.dot(hh, w_gu, preferred_element_type=F32)
            gg = gu[:, :w]
            a_scr[r, w * c:w * (c + 1)] = (gg * _sigmoid(gg) * gu[:, w:]).astype(BF16)
    x2 = [v + jnp.dot(a_scr[r, :], wfo_ref[...], preferred_element_type=F32) for r, v in zip(halves, x1)]

    h3 = [_rms(v, nple_ref[...]).astype(BF16) for v in x2]
    for r, hh, v in zip(halves, h3, x2):
        gate = _sigmoid(jnp.dot(hh, wpgate_ref[...], preferred_element_type=F32))
        pe = jnp.dot(p_ref[r, :].astype(BF16), wple_ref[...], preferred_element_type=F32)
        out_ref[r, :] = _rms(v + gate * pe, nfin_ref[...])


def _out_call(x2, p2, oa, og, act, weights, gains):
    T = x2.shape[0]
    tm = TM_OUT
    full = lambda a: pl.BlockSpec(a.shape, lambda i: (0,) * a.ndim)
    rows = lambda n: pl.BlockSpec((tm, n), lambda i: (i, 0))
    return pl.pallas_call(
        _out_body,
        grid=(T // tm,),
        in_specs=[rows(D_MODEL), rows(PLE_DIM), rows(N_QA), rows(GLA_VAL_DIM), rows(N_ACT)]
                 + [full(w) for w in weights] + [full(gn) for gn in gains],
        out_specs=rows(D_MODEL),
        out_shape=jax.ShapeDtypeStruct((T, D_MODEL), F32),
        scratch_shapes=[pltpu.VMEM((tm, D_FF), BF16)],
        compiler_params=pltpu.CompilerParams(
            dimension_semantics=("arbitrary",), vmem_limit_bytes=VMEM_LIMIT),
        name="merge_ffn",
    )(x2, p2, oa, og, act, *weights, *gains)


def _t5_bucket_table():
    q_loc = np.arange(BLOCK)[:, None] + BLOCK
    k_loc = np.arange(2 * BLOCK)[None, :]
    dist = np.maximum(q_loc - k_loc, 0)
    max_exact = NUM_BUCKETS // 2
    d_f = np.maximum(dist, max_exact).astype(np.float32)
    large = max_exact + (np.log(d_f / max_exact) / math.log(MAX_DISTANCE / max_exact)
                         * (NUM_BUCKETS - max_exact)).astype(np.int32)
    large = np.minimum(large, NUM_BUCKETS - 1)
    bucket = np.where(dist < max_exact, dist, large)
    raw = q_loc - k_loc
    in_window = (raw >= 0) & (raw < WINDOW)
    return bucket.astype(np.int32), in_window


def _level_table():
    r = np.arange(GLA_CHUNK)[:, None]
    c = np.arange(GLA_CHUNK)[None, :]
    x = np.bitwise_xor(r, c)
    lv = np.where(x > 0, np.floor(np.log2(np.maximum(x, 1))).astype(np.int32), GLA_LEVELS)
    return np.where(r >= c, lv, -1).astype(np.int32)


def _attn_bias(rel_table, sinks):
    bucket, in_window = _t5_bucket_table()
    bucket = jnp.asarray(bucket)
    rel = rel_table.astype(F32)
    bias = jnp.zeros((ATTN_HEADS, BLOCK, 2 * BLOCK), F32)
    for b in range(NUM_BUCKETS):
        bias = jnp.where(bucket[None] == b, rel[b][:, None, None], bias)
    bias = jnp.where(jnp.asarray(in_window)[None], bias, NEG)
    col0 = jnp.asarray(np.arange(2 * BLOCK) == 0)[None, None, :]
    return jnp.where(col0, sinks.astype(F32)[:, None, None], bias)


def kernel(x, p, w_in, w_gk2, b_gk, sinks, rel_table, w_proj_attn, w_proj_gla, gla_norm, w_out,
           norm_mix, norm_ffn, w_ffn_in, w_ffn_out, norm_ple, w_ple_gate, w_ple, norm_final):
    B, S, D = x.shape
    T = B * S
    assert w_in.shape[0] == 1, "single-layer trunk only"
    assert S % TM_MIX == 0 and T % TM_OUT == 0
    i = 0
    lv = jnp.asarray(_level_table())
    bias = _attn_bias(rel_table, sinks[i])

    xs = x.reshape(T, D)
    w_all = w_in[i].astype(BF16)
    w_act = w_all[:, GK0 + GLA_GATE_RANK:] * 0.5
    w_gk2p = jnp.pad(w_gk2[i], ((0, LANES - GLA_GATE_RANK), (0, 0))).astype(BF16)

    oa, og, act = _mixer_call(
        xs, S, norm_mix[i].reshape(1, D), w_all, w_act, w_gk2p, b_gk[i].reshape(1, -1), bias, lv)

    weights = [
        w_proj_attn[i].astype(BF16), w_proj_gla[i].astype(BF16), w_out[i].astype(BF16),
        w_ffn_in[i].astype(BF16), w_ffn_out[i].astype(BF16), w_ple_gate[i].astype(BF16), w_ple[i].astype(BF16),
    ]
    gains = [gla_norm[i].reshape(1, -1), norm_ffn[i].reshape(1, D), norm_ple[i].reshape(1, D),
             norm_final.reshape(1, D)]
    out = _out_call(xs, p[i].reshape(T, PLE_DIM), oa, og, act, weights, gains)
    return out.reshape(B, S, D)
```

```python
import math
from functools import partial

import numpy as np
import jax
import jax.numpy as jnp
from jax import lax
from jax.experimental import pallas as pl
from jax.experimental.pallas import tpu as pltpu

F32 = jnp.float32
BF16 = jnp.bfloat16

D_MODEL = 1024
ATTN_HEADS = 8
ATTN_KV_HEADS = 2
ATTN_GROUP = ATTN_HEADS // ATTN_KV_HEADS
HEAD_DIM = 64
WINDOW = 128
BLOCK = 128
NUM_BUCKETS = 32
MAX_DISTANCE = 128
GLA_HEADS = 4
GLA_KEY_DIM = D_MODEL // 2
GLA_VAL_DIM = D_MODEL
GLA_HEAD_K = GLA_KEY_DIM // GLA_HEADS
GLA_HEAD_V = GLA_VAL_DIM // GLA_HEADS
GLA_GATE_RANK = 16
GLA_GATE_TAU = 16.0
D_FF = -(-8 * D_MODEL // (3 * 256)) * 256
PLE_DIM = 256
EPS = 1e-6
NEG = -1e30
LOG2E = math.log2(math.e)

LANES = 128
SUBLANES = 8

TM_MIX = 512
GLA_CHUNK = 128
GLA_LEVELS = 7
TM_OUT = 512
FFN_COLS = 256
VMEM_LIMIT = 56 * 1024 * 1024

N_QA = ATTN_HEADS * HEAD_DIM
N_QKVA = 768
SEG_QKG = (768, 1792)
SEG_VG = (1792, 2816)
GK0 = 2816
N_ACT = 3072


def _sigmoid(x):
    return 0.5 * jnp.tanh(0.5 * x) + 0.5


def _rms(x, gain):
    ms = jnp.mean(x * x, axis=-1, keepdims=True)
    return x * lax.rsqrt(ms + EPS) * gain


def _chunk_cumsum(la):
    tm, n = la.shape
    groups = tm // SUBLANES
    x = la.reshape(groups, SUBLANES, n)
    sub = lax.broadcasted_iota(jnp.int32, x.shape, 1)
    for s in (1, 2, 4):
        x = x + jnp.where(sub >= s, pltpu.roll(x, s, 1), 0.0)
    per_chunk = GLA_CHUNK // SUBLANES
    out = []
    carry = None
    for i in range(groups):
        blk = x[i]
        if i % per_chunk:
            blk = blk + carry
        out.append(blk)
        carry = jnp.broadcast_to(blk[SUBLANES - 1:SUBLANES, :], (SUBLANES, n))
    return jnp.concatenate(out, axis=0)


def _attn_probs(j, first, q_scr, kv_scr, bias_ref):
    r0 = j * BLOCK
    band_row = lax.broadcasted_iota(jnp.int32, (2 * BLOCK, LANES), 0)
    low = lax.broadcasted_iota(jnp.int32, (2 * BLOCK, LANES), 1) < HEAD_DIM
    ones = jnp.ones((2 * BLOCK, LANES), BF16)
    kband = kv_scr[r0:r0 + 2 * BLOCK, 0:LANES]
    vband = kv_scr[r0:r0 + 2 * BLOCK, LANES:2 * LANES]
    zero = jnp.zeros_like(kband)
    kband = jnp.where(band_row == 0, zero, kband)
    vband = jnp.where(band_row == 0, zero, vband)
    kroll = pltpu.roll(kband, HEAD_DIM, 1)
    vroll = pltpu.roll(vband, HEAD_DIM, 1)
    ktile = ((jnp.where(low, kband, zero), jnp.where(low, zero, kroll)),
             (jnp.where(low, kroll, zero), jnp.where(low, zero, kband)))
    if j == 0:
        col = lax.broadcasted_iota(jnp.int32, (BLOCK, 2 * BLOCK), 1)
        no_prev = jnp.logical_and(first, jnp.logical_and(col >= 1, col < BLOCK))
    probs = [[None, None], [None, None]]
    for kv in range(ATTN_KV_HEADS):
        qg = q_scr[r0:r0 + BLOCK, 2 * kv * LANES:(2 * kv + 2) * LANES]
        qg = jnp.concatenate([qg[:, :LANES], qg[:, LANES:]], axis=0)
        for odd in range(2):
            s2 = jnp.dot(qg, ktile[kv][odd].T, preferred_element_type=F32)
            ps = []
            for half in range(2):
                head = 2 * (2 * kv + half) + odd
                s = s2[half * BLOCK:(half + 1) * BLOCK] + bias_ref[head]
                if j == 0:
                    s = jnp.where(no_prev, NEG, s)
                ps.append(jnp.exp(s - jnp.max(s, axis=-1, keepdims=True)).astype(BF16))
            probs[kv][odd] = jnp.concatenate(ps, axis=0)
    vnat = jnp.concatenate([vband, ones], axis=1)
    vrol = jnp.concatenate([vroll, ones], axis=1)
    p_nat = jnp.concatenate([probs[0][0], probs[1][1]], axis=0)
    p_rol = jnp.concatenate([probs[0][1], probs[1][0]], axis=0)
    return p_nat, vnat, p_rol, vrol


def _attn_out(j, p_nat, vnat, p_rol, vrol, oa_ref):
    r0 = j * BLOCK
    low_o = lax.broadcasted_iota(jnp.int32, (BLOCK, LANES), 1) < HEAD_DIM
    od_nat = jnp.dot(p_nat, vnat, preferred_element_type=F32)
    od_rol = jnp.dot(p_rol, vrol, preferred_element_type=F32)
    for blk in range(ATTN_HEADS // 2):
        rows = slice(blk * BLOCK, (blk + 1) * BLOCK)
        o_nat = od_nat[rows, :LANES] / od_nat[rows, LANES:]
        o_rol = od_rol[rows, :LANES] / od_rol[rows, LANES:]
        even, odd_ = (o_nat, o_rol) if blk < ATTN_HEADS // 4 else (o_rol, o_nat)
        oa_ref[r0:r0 + BLOCK, blk * LANES:(blk + 1) * LANES] = jnp.where(low_o, even, odd_).astype(BF16)


def _anchor_small(grow, b):
    C = GLA_CHUNK

    def row(r):
        return jnp.broadcast_to(grow(r), (SUBLANES, LANES))

    sub = lax.broadcasted_iota(jnp.int32, (SUBLANES, LANES), 0)
    parts = []
    for i in range(C // SUBLANES):
        r0 = i * SUBLANES
        if b == 4:
            a = row(r0 + 3)
        elif b == 2:
            a = jnp.where(sub < 4, row(r0 + 1), row(r0 + 5))
        else:
            a = jnp.where(sub < 2, row(r0), jnp.where(sub < 4, row(r0 + 2),
                                                      jnp.where(sub < 6, row(r0 + 4), row(r0 + 6))))
        parts.append(a)
    return jnp.concatenate(parts, axis=0)


def _level_operand(q, k, g, grow, b, row):
    C = GLA_CHUNK
    if b >= SUBLANES:
        expo, src = [], []
        for p in range(C // (2 * b)):
            lo, mid, hi = 2 * b * p, 2 * b * p + b, 2 * b * (p + 1)
            a = jnp.broadcast_to(grow(mid - 1), (b, LANES))
            expo += [a - g[lo:mid], g[mid:hi] - a]
            src += [k[lo:mid], q[mid:hi]]
        expo = jnp.concatenate(expo, axis=0)
        src = jnp.concatenate(src, axis=0)
    else:
        upper = (row & b) != 0
        d = g - _anchor_small(grow, b)
        expo = jnp.where(upper, d, -d)
        src = jnp.where(upper, q, k)
    r = (src * jnp.exp2(expo)).astype(BF16)
    return r, r.T


def _gla_scores(c, heads, qkg_scr, vg_scr, g_scr, lv):
    C = GLA_CHUNK
    r0 = c * C
    row = lax.broadcasted_iota(jnp.int32, (C, LANES), 0)
    per_head = []
    for h in heads:
        cols = slice(h * GLA_HEAD_K, (h + 1) * GLA_HEAD_K)
        grow = lambda r, cols=cols: g_scr[r0 + r:r0 + r + 1, cols]
        q_b = qkg_scr[r0:r0 + C, cols]
        k_b = qkg_scr[r0:r0 + C, GLA_KEY_DIM + h * GLA_HEAD_K:GLA_KEY_DIM + (h + 1) * GLA_HEAD_K]
        q = q_b.astype(F32)
        k = k_b.astype(F32)
        v = vg_scr[r0:r0 + C, h * GLA_HEAD_V:(h + 1) * GLA_HEAD_V]
        g = g_scr[r0:r0 + C, cols]
        glast = jnp.broadcast_to(grow(C - 1), (C, GLA_HEAD_K))

        scores = jnp.where(lv == GLA_LEVELS, jnp.dot(q_b, k_b.T, preferred_element_type=F32), 0.0)
        for l in range(GLA_LEVELS):
            r, r_t = _level_operand(q, k, g, grow, 1 << l, row)
            scores = jnp.where(lv == l, jnp.dot(r, r_t, preferred_element_type=F32), scores)
        per_head.append((h, q, k, v, g, glast, scores))
    return per_head


def _gla_state(c, per_head, s_scr, og_ref):
    C = GLA_CHUNK
    r0 = c * C
    for h, q, k, v, g, glast, scores in per_head:
        state = s_scr[h]
        qt = (q * jnp.exp2(g)).astype(BF16)
        lhs = jnp.concatenate([qt, scores.astype(BF16)], axis=1)
        rhs = jnp.concatenate([state.astype(BF16), v], axis=0)
        o = jnp.dot(lhs, rhs, preferred_element_type=F32)
        og_ref[r0:r0 + C, h * GLA_HEAD_V:(h + 1) * GLA_HEAD_V] = o.astype(BF16)

        kd_t = (k * jnp.exp2(glast - g)).astype(BF16).T
        gam = jnp.exp2(glast.T)
        upd = jnp.dot(kd_t, v, preferred_element_type=F32)
        s_scr[h] = state * jnp.concatenate([gam, gam], axis=1) + upd


def _mixer_body(tiles_per_seq, x_ref, nm_ref, wf_ref, wa_ref, wgk2_ref, bgk_ref, bias_ref, lv_ref,
                oa_ref, og_ref, gate_ref, q_scr, kv_scr, qkg_scr, vg_scr, g_scr, s_scr):
    i = pl.program_id(0)
    first = (i % tiles_per_seq) == 0

    @pl.when(i == 0)
    def _():
        kv_scr[0:BLOCK, :] = jnp.zeros((BLOCK, 2 * LANES), BF16)

    @pl.when(first)
    def _():
        s_scr[...] = jnp.zeros_like(s_scr)

    h = _rms(x_ref[...], nm_ref[...]).astype(BF16)

    def proj(w_ref, c0, c1):
        return jnp.dot(h, w_ref[:, c0:c1], preferred_element_type=F32)

    w_kvgk = jnp.concatenate(
        [wf_ref[:, N_QA:N_QKVA], wf_ref[:, GK0:GK0 + LANES], jnp.zeros((D_MODEL, LANES), BF16)], axis=1)
    kvgk = jnp.dot(h, w_kvgk, preferred_element_type=F32)
    kv_scr[BLOCK:BLOCK + TM_MIX, :] = kvgk[:, :2 * LANES].astype(BF16)
    gk = kvgk[:, 2 * LANES:3 * LANES].astype(BF16)

    def decay_chunk(j):
        rows = slice(j * GLA_CHUNK, (j + 1) * GLA_CHUNK)
        z = jnp.dot(gk[rows], wgk2_ref[...], preferred_element_type=F32) + bgk_ref[...]
        la = (jnp.minimum(z, 0.0) - jnp.log(1.0 + jnp.exp(-jnp.abs(z)))) * (LOG2E / GLA_GATE_TAU)
        g_scr[rows, :] = _chunk_cumsum(la)

    q_scr[...] = (proj(wf_ref, 0, N_QA) * (HEAD_DIM ** -0.5)).astype(BF16)

    def gla_proj(j):
        if j < 2:
            a = SEG_QKG[0] + 512 * j
            r = proj(wf_ref, a, a + 512)
            if j == 0:
                r = r * (GLA_HEAD_K ** -0.5)
            qkg_scr[:, 512 * j:512 * (j + 1)] = r.astype(BF16)
        else:
            a = SEG_VG[0] + 512 * (j - 2)
            vg_scr[:, 512 * (j - 2):512 * (j - 1)] = proj(wf_ref, a, a + 512).astype(BF16)

    for j in range(TM_MIX // BLOCK):
        att = _attn_probs(j, first, q_scr, kv_scr, bias_ref)
        gla_proj(j)
        decay_chunk(j)
        _attn_out(j, *att, oa_ref)

    def act_cols(c):
        r = proj(wa_ref, 512 * c, 512 * (c + 1))
        t = jnp.tanh(r)
        s = r * (t + 1.0) if c < 2 else 0.5 * t + 0.5
        gate_ref[:, 512 * c:512 * (c + 1)] = s.astype(BF16)

    lv = lv_ref[...]
    n_act = N_ACT // 512
    chunks = TM_MIX // GLA_CHUNK
    act_of_chunk = [list(range(n_act))[j::chunks] for j in range(chunks)]
    for j in range(chunks):
        acts = list(act_of_chunk[j])
        lo = _gla_scores(j, (0, 1), qkg_scr, vg_scr, g_scr, lv)
        act_cols(acts.pop(0))
        hi = _gla_scores(j, (2, 3), qkg_scr, vg_scr, g_scr, lv)
        _gla_state(j, lo, s_scr, og_ref)
        for c in acts:
            act_cols(c)
        _gla_state(j, hi, s_scr, og_ref)

    kv_scr[0:BLOCK, :] = kv_scr[TM_MIX:TM_MIX + BLOCK, :]


def _mixer_call(x2, seq_len, nm, w_all, w_act, w_gk2, b_gk, bias, lv):
    T = x2.shape[0]
    tm = TM_MIX
    full = lambda a: pl.BlockSpec(a.shape, lambda i: (0,) * a.ndim)
    rows = lambda n: pl.BlockSpec((tm, n), lambda i: (i, 0))
    front = pl.BlockSpec((D_MODEL, GK0 + LANES), lambda i: (0, 0))
    consts = [nm, w_all, w_act, w_gk2, b_gk, bias, lv]
    return pl.pallas_call(
        partial(_mixer_body, seq_len // tm),
        grid=(T // tm,),
        in_specs=[rows(D_MODEL), full(nm), front] + [full(a) for a in consts[2:]],
        out_specs=[rows(N_QA), rows(GLA_VAL_DIM), rows(N_ACT)],
        out_shape=[
            jax.ShapeDtypeStruct((T, N_QA), BF16),
            jax.ShapeDtypeStruct((T, GLA_VAL_DIM), BF16),
            jax.ShapeDtypeStruct((T, N_ACT), BF16),
        ],
        scratch_shapes=[
            pltpu.VMEM((tm, N_QA), BF16),
            pltpu.VMEM((BLOCK + tm, 2 * LANES), BF16),
            pltpu.VMEM((tm, 2 * GLA_KEY_DIM), BF16),
            pltpu.VMEM((tm, GLA_VAL_DIM), BF16),
            pltpu.VMEM((tm, GLA_KEY_DIM), F32),
            pltpu.VMEM((GLA_HEADS, GLA_HEAD_K, GLA_HEAD_V), F32),
        ],
        compiler_params=pltpu.CompilerParams(
            dimension_semantics=("arbitrary",), vmem_limit_bytes=VMEM_LIMIT),
        name="mixers",
    )(x2, *consts)


def _out_body(x_ref, p_ref, oa_ref, og_ref, gate_ref,
              wpa_ref, wpg_ref, wo_ref, wfi_ref, wfo_ref, wpgate_ref, wple_ref,
              gn_ref, nffn_ref, nple_ref, nfin_ref, out_ref, a_scr):
    halves = [slice(0, TM_OUT // 2), slice(TM_OUT // 2, TM_OUT)]
    w = FFN_COLS

    x1 = []
    for r in halves:
        ya = jnp.dot(oa_ref[r, :], wpa_ref[...], preferred_element_type=F32)
        parts = []
        for h in range(GLA_HEADS):
            cols = slice(h * GLA_HEAD_V, (h + 1) * GLA_HEAD_V)
            oh = _rms(og_ref[r, cols].astype(F32), gn_ref[...]) * gate_ref[r, cols].astype(F32)
            parts.append(oh.astype(BF16))
        yg = jnp.dot(jnp.concatenate(parts, axis=1), wpg_ref[...], preferred_element_type=F32)
        mixed = (gate_ref[r, GLA_VAL_DIM:GLA_VAL_DIM + D_MODEL].astype(F32) * ya
                 + gate_ref[r, GLA_VAL_DIM + D_MODEL:N_ACT].astype(F32) * yg)
        x1.append(x_ref[r, :] + jnp.dot(mixed.astype(BF16), wo_ref[...], preferred_element_type=F32))

    h2 = [_rms(v, nffn_ref[...]).astype(BF16) for v in x1]
    for c in range(D_FF // w):
        w_gu = jnp.concatenate(
            [wfi_ref[:, w * c:w * (c + 1)], wfi_ref[:, D_FF + w * c:D_FF + w * (c + 1)]], axis=1)
        for r, hh in zip(halves, h2):
            gu = jnp.dot(hh, w_gu, preferred_element_type=F32)
            gg = gu[:, :w]
            a_scr[r, w * c:w * (c + 1)] = (gg * _sigmoid(gg) * gu[:, w:]).astype(BF16)
    x2 = [v + jnp.dot(a_scr[r, :], wfo_ref[...], preferred_element_type=F32) for r, v in zip(halves, x1)]

    h3 = [_rms(v, nple_ref[...]).astype(BF16) for v in x2]
    for r, hh, v in zip(halves, h3, x2):
        gate = _sigmoid(jnp.dot(hh, wpgate_ref[...], preferred_element_type=F32))
        pe = jnp.dot(p_ref[r, :].astype(BF16), wple_ref[...], preferred_element_type=F32)
        out_ref[r, :] = _rms(v + gate * pe, nfin_ref[...])


def _out_call(x2, p2, oa, og, act, weights, gains):
    T = x2.shape[0]
    tm = TM_OUT
    full = lambda a: pl.BlockSpec(a.shape, lambda i: (0,) * a.ndim)
    rows = lambda n: pl.BlockSpec((tm, n), lambda i: (i, 0))
    return pl.pallas_call(
        _out_body,
        grid=(T // tm,),
        in_specs=[rows(D_MODEL), rows(PLE_DIM), rows(N_QA), rows(GLA_VAL_DIM), rows(N_ACT)]
                 + [full(w) for w in weights] + [full(gn) for gn in gains],
        out_specs=rows(D_MODEL),
        out_shape=jax.ShapeDtypeStruct((T, D_MODEL), F32),
        scratch_shapes=[pltpu.VMEM((tm, D_FF), BF16)],
        compiler_params=pltpu.CompilerParams(
            dimension_semantics=("arbitrary",), vmem_limit_bytes=VMEM_LIMIT),
        name="merge_ffn",
    )(x2, p2, oa, og, act, *weights, *gains)


def _t5_bucket_table():
    q_loc = np.arange(BLOCK)[:, None] + BLOCK
    k_loc = np.arange(2 * BLOCK)[None, :]
    dist = np.maximum(q_loc - k_loc, 0)
    max_exact = NUM_BUCKETS // 2
    d_f = np.maximum(dist, max_exact).astype(np.float32)
    large = max_exact + (np.log(d_f / max_exact) / math.log(MAX_DISTANCE / max_exact)
                         * (NUM_BUCKETS - max_exact)).astype(np.int32)
    large = np.minimum(large, NUM_BUCKETS - 1)
    bucket = np.where(dist < max_exact, dist, large)
    raw = q_loc - k_loc
    in_window = (raw >= 0) & (raw < WINDOW)
    return bucket.astype(np.int32), in_window


def _level_table():
    r = np.arange(GLA_CHUNK)[:, None]
    c = np.arange(GLA_CHUNK)[None, :]
    x = np.bitwise_xor(r, c)
    lv = np.where(x > 0, np.floor(np.log2(np.maximum(x, 1))).astype(np.int32), GLA_LEVELS)
    return np.where(r >= c, lv, -1).astype(np.int32)


def _attn_bias(rel_table, sinks):
    bucket, in_window = _t5_bucket_table()
    bucket = jnp.asarray(bucket)
    rel = rel_table.astype(F32)
    bias = jnp.zeros((ATTN_HEADS, BLOCK, 2 * BLOCK), F32)
    for b in range(NUM_BUCKETS):
        bias = jnp.where(bucket[None] == b, rel[b][:, None, None], bias)
    bias = jnp.where(jnp.asarray(in_window)[None], bias, NEG)
    col0 = jnp.asarray(np.arange(2 * BLOCK) == 0)[None, None, :]
    return jnp.where(col0, sinks.astype(F32)[:, None, None], bias)


def kernel(x, p, w_in, w_gk2, b_gk, sinks, rel_table, w_proj_attn, w_proj_gla, gla_norm, w_out,
           norm_mix, norm_ffn, w_ffn_in, w_ffn_out, norm_ple, w_ple_gate, w_ple, norm_final):
    B, S, D = x.shape
    T = B * S
    assert w_in.shape[0] == 1, "single-layer trunk only"
    assert S % TM_MIX == 0 and T % TM_OUT == 0
    i = 0
    lv = jnp.asarray(_level_table())
    bias = _attn_bias(rel_table, sinks[i])

    xs = x.reshape(T, D)
    w_all = w_in[i].astype(BF16)
    w_act = w_all[:, GK0 + GLA_GATE_RANK:] * 0.5
    w_gk2p = jnp.pad(w_gk2[i], ((0, LANES - GLA_GATE_RANK), (0, 0))).astype(BF16)

    oa, og, act = _mixer_call(
        xs, S, norm_mix[i].reshape(1, D), w_all, w_act, w_gk2p, b_gk[i].reshape(1, -1), bias, lv)

    weights = [
        w_proj_attn[i].astype(BF16), w_proj_gla[i].astype(BF16), w_out[i].astype(BF16),
        w_ffn_in[i].astype(BF16), w_ffn_out[i].astype(BF16), w_ple_gate[i].astype(BF16), w_ple[i].astype(BF16),
    ]
    gains = [gla_norm[i].reshape(1, -1), norm_ffn[i].reshape(1, D), norm_ple[i].reshape(1, D),
             norm_final.reshape(1, D)]
    out = _out_call(xs, p[i].reshape(T, PLE_DIM), oa, og, act, weights, gains)
    return out.reshape(B, S, D)
```

```python
import math
from functools import partial

import numpy as np
import jax
import jax.numpy as jnp
from jax import lax
from jax.experimental import pallas as pl
from jax.experimental.pallas import tpu as pltpu

F32 = jnp.float32
BF16 = jnp.bfloat16

D_MODEL = 1024
ATTN_HEADS = 8
ATTN_KV_HEADS = 2
ATTN_GROUP = ATTN_HEADS // ATTN_KV_HEADS
HEAD_DIM = 64
WINDOW = 128
BLOCK = 128
NUM_BUCKETS = 32
MAX_DISTANCE = 128
GLA_HEADS = 4
GLA_KEY_DIM = D_MODEL // 2
GLA_VAL_DIM = D_MODEL
GLA_HEAD_K = GLA_KEY_DIM // GLA_HEADS
GLA_HEAD_V = GLA_VAL_DIM // GLA_HEADS
GLA_GATE_RANK = 16
GLA_GATE_TAU = 16.0
D_FF = -(-8 * D_MODEL // (3 * 256)) * 256
PLE_DIM = 256
EPS = 1e-6
NEG = -1e30
LOG2E = math.log2(math.e)

LANES = 128
SUBLANES = 8

TM_MIX = 512
GLA_CHUNK = 128
GLA_LEVELS = 7
TM_OUT = 512
FFN_COLS = 256
VMEM_LIMIT = 56 * 1024 * 1024

N_QA = ATTN_HEADS * HEAD_DIM
N_QKVA = 768
SEG_QKG = (768, 1792)
SEG_VG = (1792, 2816)
GK0 = 2816
N_ACT = 3072


def _sigmoid(x):
    return 0.5 * jnp.tanh(0.5 * x) + 0.5


def _rms(x, gain):
    ms = jnp.mean(x * x, axis=-1, keepdims=True)
    return x * lax.rsqrt(ms + EPS) * gain


def _chunk_cumsum(la):
    tm, n = la.shape
    groups = tm // SUBLANES
    x = la.reshape(groups, SUBLANES, n)
    sub = lax.broadcasted_iota(jnp.int32, x.shape, 1)
    for s in (1, 2, 4):
        x = x + jnp.where(sub >= s, pltpu.roll(x, s, 1), 0.0)
    per_chunk = GLA_CHUNK // SUBLANES
    out = []
    carry = None
    for i in range(groups):
        blk = x[i]
        if i % per_chunk:
            blk = blk + carry
        out.append(blk)
        carry = jnp.broadcast_to(blk[SUBLANES - 1:SUBLANES, :], (SUBLANES, n))
    return jnp.concatenate(out, axis=0)


def _attn_probs(j, first, q_scr, kT_scr, v_scr, bias_ref):
    r0 = j * BLOCK
    band_row = lax.broadcasted_iota(jnp.int32, (2 * BLOCK, LANES), 0)
    band_col = lax.broadcasted_iota(jnp.int32, (LANES, 2 * BLOCK), 1)
    ones = jnp.ones((2 * BLOCK, LANES), BF16)
    kband_t = kT_scr[:, r0:r0 + 2 * BLOCK]
    kband_t = jnp.where(band_col == 0, jnp.zeros_like(kband_t), kband_t)
    vband = v_scr[r0:r0 + 2 * BLOCK, :]
    vband = jnp.where(band_row == 0, jnp.zeros_like(vband), vband)
    vroll = pltpu.roll(vband, HEAD_DIM, 1)
    zero = jnp.zeros((HEAD_DIM, 2 * BLOCK), BF16)
    kv_rows = (kband_t[:HEAD_DIM], kband_t[HEAD_DIM:])
    ktile = tuple((jnp.concatenate([kv_rows[kv], zero], axis=0), jnp.concatenate([zero, kv_rows[kv]], axis=0))
                  for kv in range(ATTN_KV_HEADS))
    if j == 0:
        col = lax.broadcasted_iota(jnp.int32, (BLOCK, 2 * BLOCK), 1)
        no_prev = jnp.logical_and(first, jnp.logical_and(col >= 1, col < BLOCK))
    probs = [[None, None], [None, None]]
    for kv in range(ATTN_KV_HEADS):
        qg = q_scr[r0:r0 + BLOCK, 2 * kv * LANES:(2 * kv + 2) * LANES]
        qg = jnp.concatenate([qg[:, :LANES], qg[:, LANES:]], axis=0)
        for odd in range(2):
            s2 = jnp.dot(qg, ktile[kv][odd], preferred_element_type=F32)
            ps = []
            for half in range(2):
                head = 2 * (2 * kv + half) + odd
                s = s2[half * BLOCK:(half + 1) * BLOCK] + bias_ref[head]
                if j == 0:
                    s = jnp.where(no_prev, NEG, s)
                ps.append(jnp.exp(s - jnp.max(s, axis=-1, keepdims=True)).astype(BF16))
            probs[kv][odd] = jnp.concatenate(ps, axis=0)
    vnat = jnp.concatenate([vband, ones], axis=1)
    vrol = jnp.concatenate([vroll, ones], axis=1)
    p_nat = jnp.concatenate([probs[0][0], probs[1][1]], axis=0)
    p_rol = jnp.concatenate([probs[0][1], probs[1][0]], axis=0)
    return p_nat, vnat, p_rol, vrol


def _attn_out(j, p_nat, vnat, p_rol, vrol, oa_ref):
    r0 = j * BLOCK
    low_o = lax.broadcasted_iota(jnp.int32, (BLOCK, LANES), 1) < HEAD_DIM
    od_nat = jnp.dot(p_nat, vnat, preferred_element_type=F32)
    od_rol = jnp.dot(p_rol, vrol, preferred_element_type=F32)
    for blk in range(ATTN_HEADS // 2):
        rows = slice(blk * BLOCK, (blk + 1) * BLOCK)
        o_nat = od_nat[rows, :LANES] / od_nat[rows, LANES:]
        o_rol = od_rol[rows, :LANES] / od_rol[rows, LANES:]
        even, odd_ = (o_nat, o_rol) if blk < ATTN_HEADS // 4 else (o_rol, o_nat)
        oa_ref[r0:r0 + BLOCK, blk * LANES:(blk + 1) * LANES] = jnp.where(low_o, even, odd_).astype(BF16)


def _anchor_small(grow, b):
    C = GLA_CHUNK

    def row(r):
        return jnp.broadcast_to(grow(r), (SUBLANES, LANES))

    sub = lax.broadcasted_iota(jnp.int32, (SUBLANES, LANES), 0)
    parts = []
    for i in range(C // SUBLANES):
        r0 = i * SUBLANES
        if b == 4:
            a = row(r0 + 3)
        elif b == 2:
            a = jnp.where(sub < 4, row(r0 + 1), row(r0 + 5))
        else:
            a = jnp.where(sub < 2, row(r0), jnp.where(sub < 4, row(r0 + 2),
                                                      jnp.where(sub < 6, row(r0 + 4), row(r0 + 6))))
        parts.append(a)
    return jnp.concatenate(parts, axis=0)


def _level_operand(q, k, g, grow, b, row):
    C = GLA_CHUNK
    if b >= SUBLANES:
        expo, src = [], []
        for p in range(C // (2 * b)):
            lo, mid, hi = 2 * b * p, 2 * b * p + b, 2 * b * (p + 1)
            a = jnp.broadcast_to(grow(mid - 1), (b, LANES))
            expo += [a - g[lo:mid], g[mid:hi] - a]
            src += [k[lo:mid], q[mid:hi]]
        expo = jnp.concatenate(expo, axis=0)
        src = jnp.concatenate(src, axis=0)
    else:
        upper = (row & b) != 0
        d = g - _anchor_small(grow, b)
        expo = jnp.where(upper, d, -d)
        src = jnp.where(upper, q, k)
    r = src * jnp.exp2(expo)
    return r.astype(BF16), r.T.astype(BF16)


def _gla_scores(c, heads, qkg_scr, vg_scr, g_scr, lv):
    C = GLA_CHUNK
    r0 = c * C
    row = lax.broadcasted_iota(jnp.int32, (C, LANES), 0)
    per_head = []
    for h in heads:
        cols = slice(h * GLA_HEAD_K, (h + 1) * GLA_HEAD_K)
        grow = lambda r, cols=cols: g_scr[r0 + r:r0 + r + 1, cols]
        q_b = qkg_scr[r0:r0 + C, cols]
        k_b = qkg_scr[r0:r0 + C, GLA_KEY_DIM + h * GLA_HEAD_K:GLA_KEY_DIM + (h + 1) * GLA_HEAD_K]
        q = q_b.astype(F32)
        k = k_b.astype(F32)
        v = vg_scr[r0:r0 + C, h * GLA_HEAD_V:(h + 1) * GLA_HEAD_V]
        g = g_scr[r0:r0 + C, cols]
        glast = jnp.broadcast_to(grow(C - 1), (C, GLA_HEAD_K))

        scores = jnp.where(lv == GLA_LEVELS, jnp.dot(q_b, k.T.astype(BF16), preferred_element_type=F32), 0.0)
        for l in range(GLA_LEVELS):
            r, r_t = _level_operand(q, k, g, grow, 1 << l, row)
            scores = jnp.where(lv == l, jnp.dot(r, r_t, preferred_element_type=F32), scores)
        per_head.append((h, q, k, v, g, glast, scores))
    return per_head


def _gla_state(c, per_head, s_scr, og_ref):
    C = GLA_CHUNK
    r0 = c * C
    for h, q, k, v, g, glast, scores in per_head:
        state = s_scr[h]
        qt = (q * jnp.exp2(g)).astype(BF16)
        lhs = jnp.concatenate([qt, scores.astype(BF16)], axis=1)
        rhs = jnp.concatenate([state.astype(BF16), v], axis=0)
        o = jnp.dot(lhs, rhs, preferred_element_type=F32)
        og_ref[r0:r0 + C, h * GLA_HEAD_V:(h + 1) * GLA_HEAD_V] = o.astype(BF16)

        kd_t = (k * jnp.exp2(glast - g)).T.astype(BF16)
        gam = jnp.exp2(glast.T)
        upd = jnp.dot(kd_t, v, preferred_element_type=F32)
        s_scr[h] = state * jnp.concatenate([gam, gam], axis=1) + upd


def _mixer_body(tiles_per_seq, x_ref, nm_ref, wf_ref, wa_ref, wgk2_ref, bgk_ref, bias_ref, lv_ref,
                oa_ref, og_ref, gate_ref, q_scr, kT_scr, v_scr, qkg_scr, vg_scr, g_scr, s_scr):
    i = pl.program_id(0)
    first = (i % tiles_per_seq) == 0

    @pl.when(i == 0)
    def _():
        kT_scr[:, 0:BLOCK] = jnp.zeros((LANES, BLOCK), BF16)
        v_scr[0:BLOCK, :] = jnp.zeros((BLOCK, LANES), BF16)

    @pl.when(first)
    def _():
        s_scr[...] = jnp.zeros_like(s_scr)

    h = _rms(x_ref[...], nm_ref[...]).astype(BF16)

    def proj(w_ref, c0, c1):
        return jnp.dot(h, w_ref[:, c0:c1], preferred_element_type=F32)

    w_kvgk = jnp.concatenate(
        [wf_ref[:, N_QA:N_QKVA], wf_ref[:, GK0:GK0 + LANES], jnp.zeros((D_MODEL, LANES), BF16)], axis=1)
    kvgk = jnp.dot(h, w_kvgk, preferred_element_type=F32)
    kT_scr[:, BLOCK:BLOCK + TM_MIX] = kvgk[:, :LANES].T.astype(BF16)
    v_scr[BLOCK:BLOCK + TM_MIX, :] = kvgk[:, LANES:2 * LANES].astype(BF16)
    gk = kvgk[:, 2 * LANES:3 * LANES].astype(BF16)

    def decay_chunk(j):
        rows = slice(j * GLA_CHUNK, (j + 1) * GLA_CHUNK)
        z = jnp.dot(gk[rows], wgk2_ref[...], preferred_element_type=F32) + bgk_ref[...]
        la = (jnp.minimum(z, 0.0) - jnp.log(1.0 + jnp.exp(-jnp.abs(z)))) * (LOG2E / GLA_GATE_TAU)
        g_scr[rows, :] = _chunk_cumsum(la)

    q_scr[...] = (proj(wf_ref, 0, N_QA) * (HEAD_DIM ** -0.5)).astype(BF16)

    def gla_proj(j):
        if j < 2:
            a = SEG_QKG[0] + 512 * j
            r = proj(wf_ref, a, a + 512)
            if j == 0:
                r = r * (GLA_HEAD_K ** -0.5)
            qkg_scr[:, 512 * j:512 * (j + 1)] = r.astype(BF16)
        else:
            a = SEG_VG[0] + 512 * (j - 2)
            vg_scr[:, 512 * (j - 2):512 * (j - 1)] = proj(wf_ref, a, a + 512).astype(BF16)

    for j in range(TM_MIX // BLOCK):
        att = _attn_probs(j, first, q_scr, kT_scr, v_scr, bias_ref)
        gla_proj(j)
        decay_chunk(j)
        _attn_out(j, *att, oa_ref)

    def act_cols(c):
        r = proj(wa_ref, 512 * c, 512 * (c + 1))
        t = jnp.tanh(r)
        s = r * (t + 1.0) if c < 2 else 0.5 * t + 0.5
        gate_ref[:, 512 * c:512 * (c + 1)] = s.astype(BF16)

    lv = lv_ref[...]
    n_act = N_ACT // 512
    chunks = TM_MIX // GLA_CHUNK
    act_of_chunk = [list(range(n_act))[j::chunks] for j in range(chunks)]
    for j in range(chunks):
        acts = list(act_of_chunk[j])
        lo = _gla_scores(j, (0, 1), qkg_scr, vg_scr, g_scr, lv)
        act_cols(acts.pop(0))
        hi = _gla_scores(j, (2, 3), qkg_scr, vg_scr, g_scr, lv)
        _gla_state(j, lo, s_scr, og_ref)
        for c in acts:
            act_cols(c)
        _gla_state(j, hi, s_scr, og_ref)

    kT_scr[:, 0:BLOCK] = kT_scr[:, TM_MIX:TM_MIX + BLOCK]
    v_scr[0:BLOCK, :] = v_scr[TM_MIX:TM_MIX + BLOCK, :]


def _mixer_call(x2, seq_len, nm, w_all, w_act, w_gk2, b_gk, bias, lv):
    T = x2.shape[0]
    tm = TM_MIX
    full = lambda a: pl.BlockSpec(a.shape, lambda i: (0,) * a.ndim)
    rows = lambda n: pl.BlockSpec((tm, n), lambda i: (i, 0))
    front = pl.BlockSpec((D_MODEL, GK0 + LANES), lambda i: (0, 0))
    consts = [nm, w_all, w_act, w_gk2, b_gk, bias, lv]
    return pl.pallas_call(
        partial(_mixer_body, seq_len // tm),
        grid=(T // tm,),
        in_specs=[rows(D_MODEL), full(nm), front] + [full(a) for a in consts[2:]],
        out_specs=[rows(N_QA), rows(GLA_VAL_DIM), rows(N_ACT)],
        out_shape=[
            jax.ShapeDtypeStruct((T, N_QA), BF16),
            jax.ShapeDtypeStruct((T, GLA_VAL_DIM), BF16),
            jax.ShapeDtypeStruct((T, N_ACT), BF16),
        ],
        scratch_shapes=[
            pltpu.VMEM((tm, N_QA), BF16),
            pltpu.VMEM((LANES, BLOCK + tm), BF16),
            pltpu.VMEM((BLOCK + tm, LANES), BF16),
            pltpu.VMEM((tm, 2 * GLA_KEY_DIM), BF16),
            pltpu.VMEM((tm, GLA_VAL_DIM), BF16),
            pltpu.VMEM((tm, GLA_KEY_DIM), F32),
            pltpu.VMEM((GLA_HEADS, GLA_HEAD_K, GLA_HEAD_V), F32),
        ],
        compiler_params=pltpu.CompilerParams(
            dimension_semantics=("arbitrary",), vmem_limit_bytes=VMEM_LIMIT),
        name="mixers",
    )(x2, *consts)


def _out_body(x_ref, p_ref, oa_ref, og_ref, gate_ref,
              wpa_ref, wpg_ref, wo_ref, wfi_ref, wfo_ref, wpgate_ref, wple_ref,
              gn_ref, nffn_ref, nple_ref, nfin_ref, out_ref, a_scr):
    halves = [slice(0, TM_OUT // 2), slice(TM_OUT // 2, TM_OUT)]
    w = FFN_COLS

    x1 = []
    for r in halves:
        ya = jnp.dot(oa_ref[r, :], wpa_ref[...], preferred_element_type=F32)
        parts = []
        for h in range(GLA_HEADS):
            cols = slice(h * GLA_HEAD_V, (h + 1) * GLA_HEAD_V)
            oh = _rms(og_ref[r, cols].astype(F32), gn_ref[...]) * gate_ref[r, cols].astype(F32)
            parts.append(oh.astype(BF16))
        yg = jnp.dot(jnp.concatenate(parts, axis=1), wpg_ref[...], preferred_element_type=F32)
        mixed = (gate_ref[r, GLA_VAL_DIM:GLA_VAL_DIM + D_MODEL].astype(F32) * ya
                 + gate_ref[r, GLA_VAL_DIM + D_MODEL:N_ACT].astype(F32) * yg)
        x1.append(x_ref[r, :] + jnp.dot(mixed.astype(BF16), wo_ref[...], preferred_element_type=F32))

    h2 = [_rms(v, nffn_ref[...]).astype(BF16) for v in x1]
    for c in range(D_FF // w):
        w_gu = jnp.concatenate(
            [wfi_ref[:, w * c:w * (c + 1)], wfi_ref[:, D_FF + w * c:D_FF + w * (c + 1)]], axis=1)
        for r, hh in zip(halves, h2):
            gu = jnp.dot(hh, w_gu, preferred_element_type=F32)
            gg = gu[:, :w]
            a_scr[r, w * c:w * (c + 1)] = (gg * _sigmoid(gg) * gu[:, w:]).astype(BF16)
    x2 = [v + jnp.dot(a_scr[r, :], wfo_ref[...], preferred_element_type=F32) for r, v in zip(halves, x1)]

    h3 = [_rms(v, nple_ref[...]).astype(BF16) for v in x2]
    for r, hh, v in zip(halves, h3, x2):
        gate = _sigmoid(jnp.dot(hh, wpgate_ref[...], preferred_element_type=F32))
        pe = jnp.dot(p_ref[r, :].astype(BF16), wple_ref[...], preferred_element_type=F32)
        out_ref[r, :] = _rms(v + gate * pe, nfin_ref[...])


def _out_call(x2, p2, oa, og, act, weights, gains):
    T = x2.shape[0]
    tm = TM_OUT
    full = lambda a: pl.BlockSpec(a.shape, lambda i: (0,) * a.ndim)
    rows = lambda n: pl.BlockSpec((tm, n), lambda i: (i, 0))
    return pl.pallas_call(
        _out_body,
        grid=(T // tm,),
        in_specs=[rows(D_MODEL), rows(PLE_DIM), rows(N_QA), rows(GLA_VAL_DIM), rows(N_ACT)]
                 + [full(w) for w in weights] + [full(gn) for gn in gains],
        out_specs=rows(D_MODEL),
        out_shape=jax.ShapeDtypeStruct((T, D_MODEL), F32),
        scratch_shapes=[pltpu.VMEM((tm, D_FF), BF16)],
        compiler_params=pltpu.CompilerParams(
            dimension_semantics=("arbitrary",), vmem_limit_bytes=VMEM_LIMIT),
        name="merge_ffn",
    )(x2, p2, oa, og, act, *weights, *gains)


def _t5_bucket_table():
    q_loc = np.arange(BLOCK)[:, None] + BLOCK
    k_loc = np.arange(2 * BLOCK)[None, :]
    dist = np.maximum(q_loc - k_loc, 0)
    max_exact = NUM_BUCKETS // 2
    d_f = np.maximum(dist, max_exact).astype(np.float32)
    large = max_exact + (np.log(d_f / max_exact) / math.log(MAX_DISTANCE / max_exact)
                         * (NUM_BUCKETS - max_exact)).astype(np.int32)
    large = np.minimum(large, NUM_BUCKETS - 1)
    bucket = np.where(dist < max_exact, dist, large)
    raw = q_loc - k_loc
    in_window = (raw >= 0) & (raw < WINDOW)
    return bucket.astype(np.int32), in_window


def _level_table():
    r = np.arange(GLA_CHUNK)[:, None]
    c = np.arange(GLA_CHUNK)[None, :]
    x = np.bitwise_xor(r, c)
    lv = np.where(x > 0, np.floor(np.log2(np.maximum(x, 1))).astype(np.int32), GLA_LEVELS)
    return np.where(r >= c, lv, -1).astype(np.int32)


def _attn_bias(rel_table, sinks):
    bucket, in_window = _t5_bucket_table()
    bucket = jnp.asarray(bucket)
    rel = rel_table.astype(F32)
    bias = jnp.zeros((ATTN_HEADS, BLOCK, 2 * BLOCK), F32)
    for b in range(NUM_BUCKETS):
        bias = jnp.where(bucket[None] == b, rel[b][:, None, None], bias)
    bias = jnp.where(jnp.asarray(in_window)[None], bias, NEG)
    col0 = jnp.asarray(np.arange(2 * BLOCK) == 0)[None, None, :]
    return jnp.where(col0, sinks.astype(F32)[:, None, None], bias)


def kernel(x, p, w_in, w_gk2, b_gk, sinks, rel_table, w_proj_attn, w_proj_gla, gla_norm, w_out,
           norm_mix, norm_ffn, w_ffn_in, w_ffn_out, norm_ple, w_ple_gate, w_ple, norm_final):
    B, S, D = x.shape
    T = B * S
    assert w_in.shape[0] == 1, "single-layer trunk only"
    assert S % TM_MIX == 0 and T % TM_OUT == 0
    i = 0
    lv = jnp.asarray(_level_table())
    bias = _attn_bias(rel_table, sinks[i])

    xs = x.reshape(T, D)
    w_all = w_in[i].astype(BF16)
    w_act = w_all[:, GK0 + GLA_GATE_RANK:] * 0.5
    w_gk2p = jnp.pad(w_gk2[i], ((0, LANES - GLA_GATE_RANK), (0, 0))).astype(BF16)

    oa, og, act = _mixer_call(
        xs, S, norm_mix[i].reshape(1, D), w_all, w_act, w_gk2p, b_gk[i].reshape(1, -1), bias, lv)

    weights = [
        w_proj_attn[i].astype(BF16), w_proj_gla[i].astype(BF16), w_out[i].astype(BF16),
        w_ffn_in[i].astype(BF16), w_ffn_out[i].astype(BF16), w_ple_gate[i].astype(BF16), w_ple[i].astype(BF16),
    ]
    gains = [gla_norm[i].reshape(1, -1), norm_ffn[i].reshape(1, D), norm_ple[i].reshape(1, D),
             norm_final.reshape(1, D)]
    out = _out_call(xs, p[i].reshape(T, PLE_DIM), oa, og, act, weights, gains)
    return out.reshape(B, S, D)
```
